```python
import math
import jax, jax.numpy as jnp
from jax import lax
import numpy as np

D_MODEL = 1024
BATCH = 2
SEQ = 16384
DEPTH = 1
DEC_BATCH = 16
DEC_SEQ = 16
PAST_LEN = 1024

CHUNK = 64
Q_BLOCK = 128
RMS_EPS = 1e-6
NEG_INF = -1e30
DELTA_HEADS = D_MODEL // 128
DELTA_DK = 128
DELTA_DV = 128
CONV_W = 4
DELTA_QKV_W = DELTA_HEADS * (2 * DELTA_DK + DELTA_DV)
DIFF_HEADS = D_MODEL // 128
DIFF_QK_DIM = 64
DIFF_V_DIM = 2 * DIFF_QK_DIM
IN_SPLIT_SIZES = (DELTA_QKV_W, DELTA_HEADS * DELTA_DV, DELTA_HEADS, DELTA_HEADS,
                  DIFF_HEADS * 2 * DIFF_QK_DIM, DIFF_HEADS * 2 * DIFF_QK_DIM, DIFF_HEADS * DIFF_V_DIM,
                  D_MODEL, D_MODEL)
IN_W = DELTA_QKV_W + DELTA_HEADS * DELTA_DV + 2 * DELTA_HEADS + DIFF_HEADS * (4 * DIFF_QK_DIM + DIFF_V_DIM) + 2 * D_MODEL
PEER_HEADS = 8
N_KEYS = 128
N_EXPERTS = N_KEYS * N_KEYS
PEER_DQ = 256
PEER_DHALF = PEER_DQ // 2
PEER_TOPK = 16
PEER_BLOCK = 256

kernel_name = 'hybrid_stream_gdn_diffattn_peer'


def rmsnorm(x, g):
    xf = x.astype(jnp.float32)
    y = xf * lax.rsqrt(jnp.mean(xf * xf, axis=-1, keepdims=True) + RMS_EPS) * g.astype(jnp.float32)
    return y.astype(x.dtype)


def l2norm(x):
    xf = x.astype(jnp.float32)
    return xf * lax.rsqrt(jnp.sum(xf * xf, axis=-1, keepdims=True) + RMS_EPS)


def gated_delta_rule(q, k, v, g, beta, s0):
    b, l, h, _ = q.shape
    dv = v.shape[-1]
    c = min(CHUNK, l)
    n = l // c
    f32 = jnp.float32

    def blocks(t):
        t = t.astype(f32).reshape((b, n, c, h) + t.shape[3:])
        return jnp.moveaxis(t, 3, 1)

    q, k, v, g, beta = (blocks(t) for t in (q, k, v, g, beta))
    gcum = jnp.cumsum(g, axis=-1)
    incl = jnp.tril(jnp.ones((c, c), bool))
    strict = jnp.tril(jnp.ones((c, c), bool), -1)
    diff = gcum[..., :, None] - gcum[..., None, :]
    decay = jnp.where(incl, jnp.exp(jnp.where(incl, diff, 0.0)), 0.0)
    kb = k * beta[..., None]
    vb = v * beta[..., None]
    lmat = jnp.where(strict, jnp.einsum('bhnid,bhnjd->bhnij', kb, k) * decay, 0.0)
    eye = jnp.eye(c, dtype=f32)
    tmat = lax.linalg.triangular_solve(lmat + eye, jnp.broadcast_to(eye, lmat.shape),
                                       left_side=True, lower=True, unit_diagonal=True)
    w = jnp.einsum('bhnij,bhnjd->bhnid', tmat, kb * jnp.exp(gcum)[..., None])
    u = jnp.einsum('bhnij,bhnjd->bhnid', tmat, vb)
    intra = jnp.einsum('bhnid,bhnjd->bhnij', q, k) * decay
    qg = q * jnp.exp(gcum)[..., None]
    glast = gcum[..., -1]
    kd = k * jnp.exp(glast[..., None] - gcum)[..., None]

    def step(s, xs):
        w_n, u_n, qg_n, intra_n, kd_n, gl_n = xs
        v_new = u_n - jnp.einsum('bhcd,bhde->bhce', w_n, s)
        o_n = jnp.einsum('bhcd,bhde->bhce', qg_n, s) + jnp.einsum('bhij,bhje->bhie', intra_n, v_new)
        s = s * jnp.exp(gl_n)[..., None, None] + jnp.einsum('bhcd,bhce->bhde', kd_n, v_new)
        return s, o_n

    xs = tuple(jnp.moveaxis(t, 2, 0) for t in (w, u, qg, intra, kd, glast))
    s_fin, o = lax.scan(step, s0.astype(f32), xs)
    o = jnp.transpose(o, (1, 0, 3, 2, 4)).reshape(b, l, h, dv)
    return o, s_fin


def diff_softmax_mix(q, k, v, lam, mask):
    s = jnp.einsum('bqhmd,bkhmd->bhmqk', q, k).astype(jnp.float32) * (DIFF_QK_DIM ** -0.5)
    if mask is not None:
        s = jnp.where(mask, s, NEG_INF)
    p = jax.nn.softmax(s, axis=-1)
    wgt = p[:, :, 0] - lam * p[:, :, 1]
    return jnp.einsum('bhqk,bkhd->bqhd', wgt.astype(v.dtype), v)


def diff_attn_prompt(q, k, v, lam):
    b, l = q.shape[:2]
    nb = l // Q_BLOCK
    qb = jnp.moveaxis(q.reshape((b, nb, Q_BLOCK) + q.shape[2:]), 1, 0)
    key_chunk = jnp.arange(l) // CHUNK

    def one(args):
        i, qi = args
        q_chunk = (i * Q_BLOCK + jnp.arange(Q_BLOCK)) // CHUNK
        mask = key_chunk[None, :] <= q_chunk[:, None]
        return diff_softmax_mix(qi, k, v, lam, mask)

    out = lax.map(one, (jnp.arange(nb), qb))
    return jnp.moveaxis(out, 0, 1).reshape(b, l, DIFF_HEADS, DIFF_V_DIM)


def peer(xn, w_q, sub_keys, u_tab, v_tab):
    b, l, d = xn.shape
    t = b * l
    nblk = -(-t // PEER_BLOCK)
    xt = jnp.pad(xn.reshape(t, d), ((0, nblk * PEER_BLOCK - t), (0, 0))).reshape(nblk, PEER_BLOCK, d)

    def one(xb):
        qh = (xb @ w_q).reshape(PEER_BLOCK, PEER_HEADS, 2, PEER_DHALF)
        s = jnp.einsum('thpd,hpnd->thpn', qh, sub_keys).astype(jnp.float32)
        sv, si = lax.top_k(s, PEER_TOPK)
        cand = (sv[:, :, 0, :, None] + sv[:, :, 1, None, :]).reshape(PEER_BLOCK, PEER_HEADS, PEER_TOPK * PEER_TOPK)
        cid = (si[:, :, 0, :, None] * N_KEYS + si[:, :, 1, None, :]).reshape(PEER_BLOCK, PEER_HEADS, PEER_TOPK * PEER_TOPK)
        top_v, top_pos = lax.top_k(cand, PEER_TOPK)
        eid = jnp.take_along_axis(cid, top_pos, axis=-1)
        gate = jax.nn.softmax(top_v, axis=-1)
        ue = u_tab[eid]
        ve = v_tab[eid]
        act = jax.nn.gelu(jnp.einsum('td,thkd->thk', xb, ue).astype(jnp.float32), approximate=False)
        return jnp.einsum('thk,thkd->td', (gate * act).astype(ve.dtype), ve)

    out = lax.map(one, xt).reshape(nblk * PEER_BLOCK, d)[:t]
    return out.reshape(b, l, d)


def layer(x, conv_buf, s0, past_k, past_v, lam_init, p):
    (norm_mix_g, w_in, conv_w, a_log, dt_bias, delta_norm_g, q_norm_g, k_norm_g,
     lq1, lk1, lq2, lk2, diff_norm_g, w_out, norm_ffn_g, peer_w_q, peer_sub_keys, peer_u, peer_v) = p
    b, l, _ = x.shape
    xn = rmsnorm(x, norm_mix_g)
    proj = xn @ w_in
    splits, acc = [], 0
    for sz in IN_SPLIT_SIZES[:-1]:
        acc += sz
        splits.append(acc)
    qkv, z, beta_raw, a_raw, fq, fk, fv, gate_a, gate_b = jnp.split(proj, splits, axis=-1)

    ext = jnp.concatenate([conv_buf.astype(qkv.dtype), qkv], axis=1)
    new_conv = ext[:, -(CONV_W - 1):]
    conv = conv_w[0] * ext[:, 0:l]
    for j in range(1, CONV_W):
        conv = conv + conv_w[j] * ext[:, j:j + l]
    qkv_c = jax.nn.silu(conv)
    dq, dk, dv = jnp.split(qkv_c, [DELTA_HEADS * DELTA_DK, 2 * DELTA_HEADS * DELTA_DK], axis=-1)
    dq = l2norm(dq.reshape(b, l, DELTA_HEADS, DELTA_DK)) * (DELTA_DK ** -0.5)
    dk = l2norm(dk.reshape(b, l, DELTA_HEADS, DELTA_DK))
    dv = dv.reshape(b, l, DELTA_HEADS, DELTA_DV)
    beta = jax.nn.sigmoid(beta_raw.astype(jnp.float32))
    g = -jnp.exp(a_log.astype(jnp.float32)) * jax.nn.softplus(a_raw.astype(jnp.float32) + dt_bias.astype(jnp.float32))
    o_a, s_new = gated_delta_rule(dq, dk, dv, g, beta, s0)
    o_a = rmsnorm(o_a, delta_norm_g) * jax.nn.silu(z.reshape(b, l, DELTA_HEADS, DELTA_DV).astype(jnp.float32))
    o_a = o_a.reshape(b, l, DELTA_HEADS * DELTA_DV).astype(x.dtype)

    fq = rmsnorm(fq.reshape(b, l, DIFF_HEADS, 2, DIFF_QK_DIM), q_norm_g)
    fk = rmsnorm(fk.reshape(b, l, DIFF_HEADS, 2, DIFF_QK_DIM), k_norm_g)
    fv = fv.reshape(b, l, DIFF_HEADS, DIFF_V_DIM)
    lam = (jnp.exp(jnp.sum(lq1.astype(jnp.float32) * lk1.astype(jnp.float32)))
           - jnp.exp(jnp.sum(lq2.astype(jnp.float32) * lk2.astype(jnp.float32))) + lam_init)
    if past_k is None:
        o_b = diff_attn_prompt(fq, fk, fv, lam)
    else:
        keys = jnp.concatenate([past_k.astype(fk.dtype), fk], axis=1)
        vals = jnp.concatenate([past_v.astype(fv.dtype), fv], axis=1)
        o_b = diff_softmax_mix(fq, keys, vals, lam, None)
    o_b = (rmsnorm(o_b, diff_norm_g) * (1.0 - lam_init)).reshape(b, l, DIFF_HEADS * DIFF_V_DIM)

    merged = jax.nn.sigmoid(gate_a) * o_a + jax.nn.sigmoid(gate_b) * o_b
    h = x + (merged @ w_out).astype(x.dtype)
    y = h + peer(rmsnorm(h, norm_ffn_g), peer_w_q, peer_sub_keys, peer_u, peer_v).astype(x.dtype)
    return y, fk, fv, s_new, new_conv


def setup_inputs(seed: int = 0) -> dict:
    key = jax.random.key(seed)
    ks = jax.random.split(key, 32)
    f32 = jnp.float32

    def nrm(k, shape, scale):
        return scale * jax.random.normal(k, shape, f32)

    def gain(k, n):
        return 1.0 + 0.02 * jax.random.normal(k, (DEPTH, n), f32)

    dt = jnp.exp(jax.random.uniform(ks[8], (DEPTH, DELTA_HEADS), f32, math.log(1e-3), math.log(1e-1)))
    return {
        'x_prompt': nrm(ks[0], (BATCH, SEQ, D_MODEL), 1.0),
        'x_sample': nrm(ks[1], (DEC_BATCH, DEC_SEQ, D_MODEL), 1.0),
        'cache_diff_k': nrm(ks[2], (DEPTH, DEC_BATCH, PAST_LEN, DIFF_HEADS, 2, DIFF_QK_DIM), 1.0),
        'cache_diff_v': nrm(ks[3], (DEPTH, DEC_BATCH, PAST_LEN, DIFF_HEADS, DIFF_V_DIM), 1.0),
        'state_delta_s': nrm(ks[4], (DEPTH, DEC_BATCH, DELTA_HEADS, DELTA_DK, DELTA_DV), 0.1),
        'state_delta_conv': nrm(ks[5], (DEPTH, DEC_BATCH, CONV_W - 1, DELTA_QKV_W), 1.0),
        'norm_mix_g': gain(ks[6], D_MODEL),
        'w_in': nrm(ks[7], (DEPTH, D_MODEL, IN_W), D_MODEL ** -0.5),
        'conv_w': nrm(ks[9], (DEPTH, CONV_W, DELTA_QKV_W), CONV_W ** -0.5),
        'delta_a_log': jnp.log(jax.random.uniform(ks[10], (DEPTH, DELTA_HEADS), f32, 1.0, 16.0)),
        'delta_dt_bias': dt + jnp.log(-jnp.expm1(-dt)),
        'delta_norm_g': gain(ks[11], DELTA_DV),
        'diff_q_norm_g': gain(ks[12], DIFF_QK_DIM),
        'diff_k_norm_g': gain(ks[13], DIFF_QK_DIM),
        'diff_lambda_q1': nrm(ks[14], (DEPTH, DIFF_QK_DIM), 0.1),
        'diff_lambda_k1': nrm(ks[15], (DEPTH, DIFF_QK_DIM), 0.1),
        'diff_lambda_q2': nrm(ks[16], (DEPTH, DIFF_QK_DIM), 0.1),
        'diff_lambda_k2': nrm(ks[17], (DEPTH, DIFF_QK_DIM), 0.1),
        'diff_norm_g': gain(ks[18], DIFF_V_DIM),
        'w_out': nrm(ks[19], (DEPTH, D_MODEL, D_MODEL), D_MODEL ** -0.5),
        'norm_ffn_g': gain(ks[20], D_MODEL),
        'peer_w_q': nrm(ks[21], (DEPTH, D_MODEL, PEER_HEADS * PEER_DQ), D_MODEL ** -0.5),
        'peer_sub_keys': nrm(ks[22], (DEPTH, PEER_HEADS, 2, N_KEYS, PEER_DHALF), PEER_DHALF ** -0.5),
        'peer_u': nrm(ks[23], (DEPTH, N_EXPERTS, D_MODEL), D_MODEL ** -0.5),
        'peer_v': nrm(ks[24], (DEPTH, N_EXPERTS, D_MODEL), D_MODEL ** -0.5),
    }


def reference(x_prompt, x_sample, cache_diff_k, cache_diff_v, state_delta_s, state_delta_conv,
              norm_mix_g, w_in, conv_w, delta_a_log, delta_dt_bias, delta_norm_g,
              diff_q_norm_g, diff_k_norm_g, diff_lambda_q1, diff_lambda_k1, diff_lambda_q2, diff_lambda_k2,
              diff_norm_g, w_out, norm_ffn_g, peer_w_q, peer_sub_keys, peer_u, peer_v):
    yp, ys = x_prompt, x_sample
    kp, vp, sp, cp = [], [], [], []
    kq, vq, sq, cq = [], [], [], []
    for l in range(DEPTH):
        p = (norm_mix_g[l], w_in[l], conv_w[l], delta_a_log[l], delta_dt_bias[l], delta_norm_g[l],
             diff_q_norm_g[l], diff_k_norm_g[l], diff_lambda_q1[l], diff_lambda_k1[l],
             diff_lambda_q2[l], diff_lambda_k2[l], diff_norm_g[l], w_out[l], norm_ffn_g[l],
             peer_w_q[l], peer_sub_keys[l], peer_u[l], peer_v[l])
        lam_init = 0.8 - 0.6 * math.exp(-0.3 * l)
        conv0 = jnp.zeros((yp.shape[0], CONV_W - 1, DELTA_QKV_W), yp.dtype)
        s0 = jnp.zeros((yp.shape[0], DELTA_HEADS, DELTA_DK, DELTA_DV), jnp.float32)
        yp, k_new, v_new, s_new, c_new = layer(yp, conv0, s0, None, None, lam_init, p)
        kp.append(k_new); vp.append(v_new); sp.append(s_new); cp.append(c_new)
        ys, k_new, v_new, s_new, c_new = layer(ys, state_delta_conv[l], state_delta_s[l],
                                               cache_diff_k[l], cache_diff_v[l], lam_init, p)
        kq.append(k_new); vq.append(v_new); sq.append(s_new); cq.append(c_new)
    return (yp, ys,
            jnp.stack(kp), jnp.stack(vp), jnp.stack(sp), jnp.stack(cp),
            jnp.stack(kq), jnp.stack(vq), jnp.stack(sq), jnp.stack(cq))
```

```python
import functools
import math

import jax
import jax.numpy as jnp
from jax import lax
from jax.experimental import pallas as pl
from jax.experimental.pallas import tpu as pltpu

F32 = jnp.float32
BF16 = jnp.bfloat16

RMS_EPS = 1e-6
NEG_INF = -1e30
CHUNK = 64
HEAD_W = 128
N_HEADS = 8
DIFF_QK = 64
CONV_W = 4
PEER_HEADS = 8
N_KEYS = 128
PEER_TOPK = 16
LANES = 128
SUBLANES = 8
VMEM_LIMIT = 56 * 1024 * 1024


def _cparams(sem):
    return pltpu.CompilerParams(dimension_semantics=sem, vmem_limit_bytes=VMEM_LIMIT)


def _dot(a, b):
    return jnp.dot(a, b, preferred_element_type=F32)


def _dot_nt(a, b):
    return lax.dot_general(a, b, (((1,), (1,)), ((), ())), preferred_element_type=F32)


def _dot_tn(a, b):
    return lax.dot_general(a, b, (((0,), (0,)), ((), ())), preferred_element_type=F32)


def _sigmoid(x):
    return 1.0 / (1.0 + jnp.exp(-x))


def _silu(x):
    return x * _sigmoid(x)


def _in_proj_kernel(x_ref, g_ref, w_ref, wba_ref, o_ref, oba_ref, xn_ref):
    @pl.when(pl.program_id(1) == 0)
    def _():
        x = x_ref[...]
        r = lax.rsqrt(jnp.mean(x * x, axis=-1, keepdims=True) + RMS_EPS)
        xn = (x * r * g_ref[...]).astype(BF16)
        xn_ref[...] = xn
        oba_ref[...] = _dot(xn, wba_ref[...])

    o_ref[...] = _dot(xn_ref[...], w_ref[...])


def _in_proj(x, g, w_main, w_ba, tm, tn):
    t, d = x.shape
    nw = w_main.shape[1]
    return pl.pallas_call(
        _in_proj_kernel,
        grid=(t // tm, nw // tn),
        in_specs=[
            pl.BlockSpec((tm, d), lambda i, j: (i, 0)),
            pl.BlockSpec((1, d), lambda i, j: (0, 0)),
            pl.BlockSpec((d, tn), lambda i, j: (0, j)),
            pl.BlockSpec((d, LANES), lambda i, j: (0, 0)),
        ],
        out_specs=[
            pl.BlockSpec((tm, tn), lambda i, j: (i, j)),
            pl.BlockSpec((tm, LANES), lambda i, j: (i, 0)),
        ],
        out_shape=[jax.ShapeDtypeStruct((t, nw), F32), jax.ShapeDtypeStruct((t, LANES), F32)],
        scratch_shapes=[pltpu.VMEM((tm, d), BF16)],
        compiler_params=_cparams(("parallel", "arbitrary")),
        name="in_proj",
    )(x, g, w_main, w_ba)


def _delta_prep_kernel(cur_ref, prev_ref, cbuf_ref, cw_ref, ba_ref, alog_ref, dtb_ref,
                       q_ref, k_ref, v_ref, beta_ref, gc_ref, *, tiles_per_seq, chunk):
    tm = cur_ref.shape[0]
    qkv_w = cur_ref.shape[1]
    d = qkv_w // 3
    first = (pl.program_id(0) % tiles_per_seq) == 0
    col_w = 4 * HEAD_W
    for c0 in range(0, qkv_w, col_w):
        cur = cur_ref[:, c0:c0 + col_w]
        prev = jnp.where(first, cbuf_ref[:, c0:c0 + col_w], prev_ref[:, c0:c0 + col_w])
        ext = jnp.concatenate([prev, cur], axis=0)
        conv = cw_ref[0:1, c0:c0 + col_w] * ext[SUBLANES - 3:SUBLANES - 3 + tm]
        for j in range(1, CONV_W):
            conv = conv + cw_ref[j:j + 1, c0:c0 + col_w] * ext[SUBLANES - 3 + j:SUBLANES - 3 + j + tm]
        act = _silu(conv)
        for hh in range(col_w // HEAD_W):
            col = c0 + hh * HEAD_W
            a = act[:, hh * HEAD_W:(hh + 1) * HEAD_W]
            if col < 2 * d:
                a = a * lax.rsqrt(jnp.sum(a * a, axis=-1, keepdims=True) + RMS_EPS)
            if col < d:
                q_ref[:, col:col + HEAD_W] = (a * (HEAD_W ** -0.5)).astype(BF16)
            elif col < 2 * d:
                k_ref[:, col - d:col - d + HEAD_W] = a.astype(BF16)
            else:
                v_ref[:, col - 2 * d:col - 2 * d + HEAD_W] = a.astype(BF16)
    ba = ba_ref[...]
    beta_ref[...] = _sigmoid(ba)
    xs = ba + dtb_ref[...]
    softplus = jnp.maximum(xs, 0.0) + jnp.log(1.0 + jnp.exp(-jnp.abs(xs)))
    g = -jnp.exp(alog_ref[...]) * softplus
    row = lax.broadcasted_iota(jnp.int32, (tm, tm), 0)
    colm = lax.broadcasted_iota(jnp.int32, (tm, tm), 1)
    tri = jnp.where((row // chunk == colm // chunk) & (colm <= row), 1.0, 0.0).astype(F32)
    gc_ref[...] = jnp.dot(tri, g, preferred_element_type=F32, precision=lax.Precision.HIGHEST)


def _delta_prep(proj, proj_ba, cbuf8, conv_w, alog_pad, dtb_pad, seq_len, tm, chunk):
    t = proj.shape[0]
    qkv_w = conv_w.shape[1]
    d = qkv_w // 3
    tiles_per_seq = seq_len // tm
    rows8 = tm // SUBLANES
    kern = functools.partial(_delta_prep_kernel, tiles_per_seq=tiles_per_seq, chunk=chunk)
    return pl.pallas_call(
        kern,
        grid=(t // tm,),
        in_specs=[
            pl.BlockSpec((tm, qkv_w), lambda i: (i, 0)),
            pl.BlockSpec((SUBLANES, qkv_w), lambda i: (jnp.maximum(i * rows8 - 1, 0), 0)),
            pl.BlockSpec((None, SUBLANES, qkv_w), lambda i: (i // tiles_per_seq, 0, 0)),
            pl.BlockSpec((CONV_W, qkv_w), lambda i: (0, 0)),
            pl.BlockSpec((tm, LANES), lambda i: (i, 0)),
            pl.BlockSpec((1, LANES), lambda i: (0, 0)),
            pl.BlockSpec((1, LANES), lambda i: (0, 0)),
        ],
        out_specs=[
            pl.BlockSpec((tm, d), lambda i: (i, 0)),
            pl.BlockSpec((tm, d), lambda i: (i, 0)),
            pl.BlockSpec((tm, d), lambda i: (i, 0)),
            pl.BlockSpec((tm, LANES), lambda i: (i, 0)),
            pl.BlockSpec((tm, LANES), lambda i: (i, 0)),
        ],
        out_shape=[jax.ShapeDtypeStruct((t, d), BF16)] * 3 + [jax.ShapeDtypeStruct((t, LANES), F32)] * 2,
        compiler_params=_cparams(("parallel",)),
        name="delta_prep",
    )(proj, proj, cbuf8, conv_w, proj_ba, alog_pad, dtb_pad)


def _attn_prep_kernel(fq_ref, fk_ref, fv_ref, gq_ref, gk_ref, qa_ref, kf_ref, ka_ref, va_ref):
    tm, d = fq_ref.shape
    lane = lax.broadcasted_iota(jnp.int32, (tm, HEAD_W), 1)
    low = lane < DIFF_QK

    def qk_norm(x, g):
        s = x * x
        tot = jnp.sum(s, axis=-1, keepdims=True)
        lo = jnp.sum(jnp.where(low, s, 0.0), axis=-1, keepdims=True)
        ms = jnp.where(low, lo, tot - lo) * (1.0 / DIFF_QK)
        return x * lax.rsqrt(ms + RMS_EPS) * g

    for h in range(d // HEAD_W):
        sl = slice(h * HEAD_W, (h + 1) * HEAD_W)
        qn = qk_norm(fq_ref[:, sl], gq_ref[...])
        qa_ref[:, sl] = (qn * (DIFF_QK ** -0.5)).astype(BF16)
        kn = qk_norm(fk_ref[:, sl], gk_ref[...])
        kf_ref[:, sl] = kn
        ka_ref[:, sl] = kn.astype(BF16)
    va_ref[...] = fv_ref[...].astype(BF16)


def _attn_prep(proj, gq2, gk2, d, tm, col_q, col_k, col_v):
    t = proj.shape[0]
    return pl.pallas_call(
        _attn_prep_kernel,
        grid=(t // tm,),
        in_specs=[
            pl.BlockSpec((tm, d), lambda i: (i, col_q)),
            pl.BlockSpec((tm, d), lambda i: (i, col_k)),
            pl.BlockSpec((tm, d), lambda i: (i, col_v)),
            pl.BlockSpec((1, HEAD_W), lambda i: (0, 0)),
            pl.BlockSpec((1, HEAD_W), lambda i: (0, 0)),
        ],
        out_specs=[pl.BlockSpec((tm, d), lambda i: (i, 0))] * 4,
        out_shape=[jax.ShapeDtypeStruct((t, d), BF16), jax.ShapeDtypeStruct((t, d), F32),
                   jax.ShapeDtypeStruct((t, d), BF16), jax.ShapeDtypeStruct((t, d), BF16)],
        compiler_params=_cparams(("parallel",)),
        name="attn_prep",
    )(proj, proj, proj, gq2, gk2)


def _delta_rule_kernel(q_ref, k_ref, v_ref, z_ref, beta_ref, gc_ref, s0_ref, ng_ref,
                       o_ref, sfin_ref, s_ref, *, chunk, n_chunks):
    n = pl.program_id(1)
    rows = q_ref.shape[0]
    n_heads = q_ref.shape[1] // HEAD_W

    @pl.when(n == 0)
    def _():
        s_ref[...] = s0_ref[...]

    lane = lax.broadcasted_iota(jnp.int32, (rows, LANES), 1)
    ri = lax.broadcasted_iota(jnp.int32, (chunk, chunk), 0)
    ci = lax.broadcasted_iota(jnp.int32, (chunk, chunk), 1)
    eye = ri == ci
    incl = ci <= ri
    strict = ci < ri
    eye_f = jnp.where(eye, 1.0, 0.0).astype(F32)
    beta_all = beta_ref[...]
    gc_all = gc_ref[...]
    n_sq = int(math.log2(chunk)) - 1

    for h in range(n_heads):
        sl = slice(h * HEAD_W, (h + 1) * HEAD_W)
        beta_h = jnp.sum(jnp.where(lane == h, beta_all, 0.0), axis=-1, keepdims=True)
        g_h = jnp.sum(jnp.where(lane == n_heads + h, gc_all, 0.0), axis=-1, keepdims=True)
        s = s_ref[h]
        for c in range(rows // chunk):
            rs = slice(c * chunk, (c + 1) * chunk)
            q = q_ref[rs, sl].astype(F32)
            k = k_ref[rs, sl].astype(F32)
            v = v_ref[rs, sl].astype(F32)
            beta = beta_h[rs]
            gcol = g_h[rs]
            grow = jnp.sum(jnp.where(eye, gcol, 0.0), axis=0, keepdims=True)
            decay = jnp.where(incl, jnp.exp(jnp.where(incl, gcol - grow, 0.0)), 0.0)
            eg = jnp.exp(gcol)
            glast = gcol[chunk - 1:chunk]
            kb = k * beta
            kb16 = kb.astype(BF16)
            k16 = k.astype(BF16)
            lmat = jnp.where(strict, _dot_nt(kb16, k16) * decay, 0.0)
            mp = -lmat
            tmat = eye_f + mp
            for _ in range(n_sq):
                mp16 = mp.astype(BF16)
                mp = _dot(mp16, mp16)
                tmat = tmat + _dot(tmat.astype(BF16), mp.astype(BF16))
            t16 = tmat.astype(BF16)
            w = _dot(t16, (kb * eg).astype(BF16))
            u = _dot(t16, (v * beta).astype(BF16))
            intra = _dot_nt(q.astype(BF16), k16) * decay
            s16 = s.astype(BF16)
            v_new = u - _dot(w.astype(BF16), s16)
            vn16 = v_new.astype(BF16)
            o = _dot((q * eg).astype(BF16), s16) + _dot(intra.astype(BF16), vn16)
            kd = k * jnp.exp(glast - gcol)
            s = s * jnp.exp(glast) + _dot_tn(kd.astype(BF16), vn16)
            on = o * lax.rsqrt(jnp.mean(o * o, axis=-1, keepdims=True) + RMS_EPS) * ng_ref[...]
            o_ref[rs, sl] = (on * _silu(z_ref[rs, sl])).astype(BF16)
        s_ref[h] = s

    @pl.when(n == n_chunks - 1)
    def _():
        sfin_ref[...] = s_ref[...]


def _delta_rule(dq, dk, dv, proj, beta, gc, s0, ng, seq_len, chunk, rows, col_z):
    t, d = dq.shape
    nb = t // seq_len
    n_steps = seq_len // rows
    n_heads = d // HEAD_W
    kern = functools.partial(_delta_rule_kernel, chunk=chunk, n_chunks=n_steps)
    tok = lambda b, n: (b * n_steps + n, 0)
    return pl.pallas_call(
        kern,
        grid=(nb, n_steps),
        in_specs=[
            pl.BlockSpec((rows, d), tok),
            pl.BlockSpec((rows, d), tok),
            pl.BlockSpec((rows, d), tok),
            pl.BlockSpec((rows, d), lambda b, n: (b * n_steps + n, col_z)),
            pl.BlockSpec((rows, LANES), tok),
            pl.BlockSpec((rows, LANES), tok),
            pl.BlockSpec((None, n_heads, HEAD_W, HEAD_W), lambda b, n: (b, 0, 0, 0)),
            pl.BlockSpec((1, HEAD_W), lambda b, n: (0, 0)),
        ],
        out_specs=[
            pl.BlockSpec((rows, d), tok),
            pl.BlockSpec((None, n_heads, HEAD_W, HEAD_W), lambda b, n: (b, 0, 0, 0)),
        ],
        out_shape=[jax.ShapeDtypeStruct((t, d), BF16),
                   jax.ShapeDtypeStruct((nb, n_heads, HEAD_W, HEAD_W), F32)],
        scratch_shapes=[pltpu.VMEM((n_heads, HEAD_W, HEAD_W), F32)],
        compiler_params=_cparams(("parallel", "arbitrary")),
        name="delta_rule",
    )(dq, dk, dv, proj, beta, gc, s0, ng)


def _lambda(lam_ref, lam_init):
    l = lam_ref[...]
    a = jnp.sum(l[0:1] * l[1:2], axis=-1, keepdims=True)
    b = jnp.sum(l[2:3] * l[3:4], axis=-1, keepdims=True)
    return jnp.exp(a) - jnp.exp(b) + lam_init


def _sub_norm(o, g, lam_init):
    return o * lax.rsqrt(jnp.mean(o * o, axis=-1, keepdims=True) + RMS_EPS) * g * (1.0 - lam_init)


def _attn_prompt_kernel(q_ref, k_ref, v_ref, lam_ref, ng_ref, o_ref, m_ref, l_ref, acc_ref, *, lam_init):
    i = pl.program_id(2)
    bq = q_ref.shape[0]
    q = q_ref[...]
    lane = lax.broadcasted_iota(jnp.int32, q.shape, 1)
    zero = jnp.zeros_like(q)
    qm = (jnp.where(lane < DIFF_QK, q, zero), jnp.where(lane >= DIFF_QK, q, zero))
    m_ref[...] = jnp.full(m_ref.shape, NEG_INF, F32)
    l_ref[...] = jnp.zeros(l_ref.shape, F32)
    acc_ref[...] = jnp.zeros(acc_ref.shape, F32)

    def block(j, mask):
        kb = k_ref[pl.ds(pl.multiple_of(j * bq, bq), bq), :]
        vb = v_ref[pl.ds(pl.multiple_of(j * bq, bq), bq), :]
        for m in range(2):
            s = _dot_nt(qm[m], kb)
            if mask is not None:
                s = jnp.where(mask, s, NEG_INF)
            m_prev = m_ref[m]
            m_new = jnp.maximum(m_prev, jnp.max(s, axis=-1, keepdims=True))
            alpha = jnp.exp(m_prev - m_new)
            p = jnp.exp(s - m_new)
            l_ref[m] = alpha * l_ref[m] + jnp.sum(p, axis=-1, keepdims=True)
            acc_ref[m] = alpha * acc_ref[m] + _dot(p.astype(BF16), vb)
            m_ref[m] = m_new

    def body(j, carry):
        block(j, None)
        return carry

    lax.fori_loop(0, i, body, 0)
    ri = lax.broadcasted_iota(jnp.int32, (bq, bq), 0)
    ci = lax.broadcasted_iota(jnp.int32, (bq, bq), 1)
    block(i, (ci // CHUNK) <= (ri // CHUNK))

    lam = _lambda(lam_ref, lam_init)
    o = acc_ref[0] / l_ref[0] - lam * (acc_ref[1] / l_ref[1])
    o_ref[...] = _sub_norm(o, ng_ref[...], lam_init).astype(BF16)


def _attn_prompt(qa, ka, va, lam4, ng, seq_len, bq, lam_init):
    t, d = qa.shape
    nb = t // seq_len
    nq = seq_len // bq
    n_heads = d // HEAD_W
    kern = functools.partial(_attn_prompt_kernel, lam_init=lam_init)
    return pl.pallas_call(
        kern,
        grid=(nb, n_heads, nq),
        in_specs=[
            pl.BlockSpec((bq, HEAD_W), lambda b, h, i: (b * nq + i, h)),
            pl.BlockSpec((seq_len, HEAD_W), lambda b, h, i: (b, h)),
            pl.BlockSpec((seq_len, HEAD_W), lambda b, h, i: (b, h)),
            pl.BlockSpec((4, DIFF_QK), lambda b, h, i: (0, 0)),
            pl.BlockSpec((1, HEAD_W), lambda b, h, i: (0, 0)),
        ],
        out_specs=pl.BlockSpec((bq, HEAD_W), lambda b, h, i: (b * nq + i, h)),
        out_shape=jax.ShapeDtypeStruct((t, d), BF16),
        scratch_shapes=[pltpu.VMEM((2, bq, 1), F32), pltpu.VMEM((2, bq, 1), F32),
                        pltpu.VMEM((2, bq, HEAD_W), F32)],
        compiler_params=_cparams(("parallel", "parallel", "arbitrary")),
        name="attn_prompt",
    )(qa, ka, va, lam4, ng)


def _attn_cached_kernel(q_ref, kn_ref, vn_ref, ck_ref, cv_ref, lam_ref, ng_ref, o_ref, *, lam_init):
    q = q_ref[...]
    lane = lax.broadcasted_iota(jnp.int32, q.shape, 1)
    zero = jnp.zeros_like(q)
    kp = ck_ref[...].astype(BF16)
    vp = cv_ref[...].astype(BF16)
    kn = kn_ref[...]
    vn = vn_ref[...]
    outs = []
    for m in range(2):
        qm = jnp.where((lane < DIFF_QK) if m == 0 else (lane >= DIFF_QK), q, zero)
        sp = _dot_nt(qm, kp)
        sn = _dot_nt(qm, kn)
        mx = jnp.maximum(jnp.max(sp, axis=-1, keepdims=True), jnp.max(sn, axis=-1, keepdims=True))
        pp = jnp.exp(sp - mx)
        pn = jnp.exp(sn - mx)
        den = jnp.sum(pp, axis=-1, keepdims=True) + jnp.sum(pn, axis=-1, keepdims=True)
        outs.append((_dot(pp.astype(BF16), vp) + _dot(pn.astype(BF16), vn)) / den)
    lam = _lambda(lam_ref, lam_init)
    o = outs[0] - lam * outs[1]
    o_ref[...] = _sub_norm(o, ng_ref[...], lam_init).astype(BF16)


def _attn_cached(qa, ka, va, cache_k, cache_v, lam4, ng, seq_len, lam_init):
    t, d = qa.shape
    nb = t // seq_len
    past = cache_k.shape[1]
    n_heads = d // HEAD_W
    kern = functools.partial(_attn_cached_kernel, lam_init=lam_init)
    tok = lambda b, h: (b, h)
    return pl.pallas_call(
        kern,
        grid=(nb, n_heads),
        in_specs=[
            pl.BlockSpec((seq_len, HEAD_W), tok),
            pl.BlockSpec((seq_len, HEAD_W), tok),
            pl.BlockSpec((seq_len, HEAD_W), tok),
            pl.BlockSpec((None, past, HEAD_W), lambda b, h: (b, 0, h)),
            pl.BlockSpec((None, past, HEAD_W), lambda b, h: (b, 0, h)),
            pl.BlockSpec((4, DIFF_QK), lambda b, h: (0, 0)),
            pl.BlockSpec((1, HEAD_W), lambda b, h: (0, 0)),
        ],
        out_specs=pl.BlockSpec((seq_len, HEAD_W), tok),
        out_shape=jax.ShapeDtypeStruct((t, d), BF16),
        compiler_params=_cparams(("parallel", "parallel")),
        name="attn_cached",
    )(qa, ka, va, cache_k, cache_v, lam4, ng)


def _merge_out_kernel(x_ref, oa_ref, ob_ref, ga_ref, gb_ref, wo_ref, g2_ref, h_ref, xnt_ref):
    merged = (_sigmoid(ga_ref[...]) * oa_ref[...].astype(F32)
              + _sigmoid(gb_ref[...]) * ob_ref[...].astype(F32))
    h = x_ref[...] + _dot(merged.astype(BF16), wo_ref[...])
    h_ref[...] = h
    hn = h * lax.rsqrt(jnp.mean(h * h, axis=-1, keepdims=True) + RMS_EPS) * g2_ref[...]
    xnt_ref[...] = hn.T.astype(BF16)


def _merge_out(x, oa, ob, proj, w_out, g2, tm, col_ga, col_gb):
    t, d = x.shape
    row = lambda i: (i, 0)
    return pl.pallas_call(
        _merge_out_kernel,
        grid=(t // tm,),
        in_specs=[
            pl.BlockSpec((tm, d), row),
            pl.BlockSpec((tm, d), row),
            pl.BlockSpec((tm, d), row),
            pl.BlockSpec((tm, d), lambda i: (i, col_ga)),
            pl.BlockSpec((tm, d), lambda i: (i, col_gb)),
            pl.BlockSpec((d, d), lambda i: (0, 0)),
            pl.BlockSpec((1, d), lambda i: (0, 0)),
        ],
        out_specs=[pl.BlockSpec((tm, d), row), pl.BlockSpec((d, tm), lambda i: (0, i))],
        out_shape=[jax.ShapeDtypeStruct((t, d), F32), jax.ShapeDtypeStruct((d, t), BF16)],
        compiler_params=_cparams(("parallel",)),
        name="merge_out",
    )(x, oa, ob, proj, proj, w_out, g2)


def _top_values(parts, count):
    vals = []
    for r in range(count):
        m = parts[0]
        for p in parts[1:]:
            m = jnp.maximum(m, p)
        m = jnp.max(m, axis=0, keepdims=True)
        vals.append(m)
        if r + 1 < count:
            parts = [jnp.where(p == m, NEG_INF, p) for p in parts]
    return vals


def _peer_route_kernel(xnt_ref, wqt_ref, sk_ref, s0_ref, s1_ref, tau_ref):
    pq = _dot(wqt_ref[...], xnt_ref[...]).astype(BF16)
    dh = sk_ref.shape[-1]
    taus = []
    for h in range(PEER_HEADS):
        sc = []
        for p in range(2):
            r0 = (h * 2 + p) * dh
            sc.append(_dot(sk_ref[h, p], pq[r0:r0 + dh]))
        sv0 = _top_values([sc[0]], PEER_TOPK)
        sv1 = jnp.concatenate(_top_values([sc[1]], PEER_TOPK), axis=0)
        cand = [sv0[i] + sv1 for i in range(PEER_TOPK)]
        tv = _top_values(cand, PEER_TOPK + 1)
        mx = tv[0]
        z = jnp.exp(tv[0] - mx)
        for r in range(1, PEER_TOPK):
            z = z + jnp.exp(tv[r] - mx)
        shift = mx + jnp.log(z)
        taus.append(0.5 * (tv[PEER_TOPK - 1] + tv[PEER_TOPK]) - shift)
        s0_ref[h] = sc[0] - shift
        s1_ref[h] = sc[1]
    tau_ref[...] = jnp.concatenate(taus, axis=0)


def _peer_route(xnt, wqt, sk16, tm):
    d, t = xnt.shape
    qw = wqt.shape[0]
    tok = lambda i: (0, 0, i)
    return pl.pallas_call(
        _peer_route_kernel,
        grid=(t // tm,),
        in_specs=[
            pl.BlockSpec((d, tm), lambda i: (0, i)),
            pl.BlockSpec((qw, d), lambda i: (0, 0)),
            pl.BlockSpec(sk16.shape, lambda i: (0, 0, 0, 0)),
        ],
        out_specs=[
            pl.BlockSpec((PEER_HEADS, N_KEYS, tm), tok),
            pl.BlockSpec((PEER_HEADS, N_KEYS, tm), tok),
            pl.BlockSpec((PEER_HEADS, tm), lambda i: (0, i)),
        ],
        out_shape=[jax.ShapeDtypeStruct((PEER_HEADS, N_KEYS, t), F32),
                   jax.ShapeDtypeStruct((PEER_HEADS, N_KEYS, t), F32),
                   jax.ShapeDtypeStruct((PEER_HEADS, t), F32)],
        compiler_params=_cparams(("parallel",)),
        name="peer_route",
    )(xnt, wqt, sk16)


def _gelu(x):
    return 0.5 * x * (1.0 + lax.erf(x * (2.0 ** -0.5)))


def _peer_mix_kernel(xnt_ref, u_ref, vt_ref, s0_ref, s1_ref, tau_ref, h_ref, y_ref, acc_ref, gw_ref,
                     *, n_eblk):
    j = pl.program_id(1)
    na = s0_ref.shape[1]
    tm = xnt_ref.shape[1]

    @pl.when(j == 0)
    def _():
        acc_ref[...] = jnp.zeros(acc_ref.shape, F32)

    act = _dot(u_ref[...], xnt_ref[...])
    for a in range(na):
        wgt = jnp.zeros((N_KEYS, tm), F32)
        for h in range(PEER_HEADS):
            c = s0_ref[h, a:a + 1, :] + s1_ref[h]
            wgt = wgt + jnp.where(c >= tau_ref[h:h + 1, :], jnp.exp(c), 0.0)
        rs = slice(a * N_KEYS, (a + 1) * N_KEYS)
        gw_ref[rs, :] = (wgt * _gelu(act[rs])).astype(BF16)
    acc_ref[...] += _dot(vt_ref[...], gw_ref[...])

    @pl.when(j == n_eblk - 1)
    def _():
        y_ref[...] = h_ref[...] + acc_ref[...].T


def _peer_mix(xnt, u16, vt16, s0z, s1, tau, h, tm, te):
    d, t = xnt.shape
    n_exp = u16.shape[0]
    na = te // N_KEYS
    n_eblk = n_exp // te
    kern = functools.partial(_peer_mix_kernel, n_eblk=n_eblk)
    return pl.pallas_call(
        kern,
        grid=(t // tm, n_eblk),
        in_specs=[
            pl.BlockSpec((d, tm), lambda i, j: (0, i)),
            pl.BlockSpec((te, d), lambda i, j: (j, 0)),
            pl.BlockSpec((d, te), lambda i, j: (0, j)),
            pl.BlockSpec((PEER_HEADS, na, tm), lambda i, j: (0, j, i)),
            pl.BlockSpec((PEER_HEADS, N_KEYS, tm), lambda i, j: (0, 0, i)),
            pl.BlockSpec((PEER_HEADS, tm), lambda i, j: (0, i)),
            pl.BlockSpec((tm, d), lambda i, j: (i, 0)),
        ],
        out_specs=pl.BlockSpec((tm, d), lambda i, j: (i, 0)),
        out_shape=jax.ShapeDtypeStruct((t, d), F32),
        scratch_shapes=[pltpu.VMEM((d, tm), F32), pltpu.VMEM((te, tm), BF16)],
        compiler_params=_cparams(("parallel", "arbitrary")),
        name="peer_mix",
    )(xnt, u16, vt16, s0z, s1, tau, h)


def _pick(n, pref):
    t = min(n, pref)
    while n % t:
        t //= 2
    return t


def _layer(x, conv_buf, s0, past_k, past_v, lam_init, wts):
    nb, seq_len, d = x.shape
    t = nb * seq_len
    n_heads = d // HEAD_W
    qkv_w = 3 * d
    x2 = x.reshape(t, d)

    tm = _pick(t, 512)
    proj, proj_ba = _in_proj(x2, wts["norm_mix_g"], wts["w_main"], wts["w_ba"], tm, 1024)
    col ={name: idx for idx, name in enumerate(("q", "k", "v", "z", "fq", "fk", "fv", "ga", "gb"))}

    chunk = min(CHUNK, seq_len)
    tp = _pick(seq_len, 256)
    cbuf8 = jnp.concatenate(
        [jnp.zeros((nb, SUBLANES - (CONV_W - 1), qkv_w), F32), conv_buf.astype(F32)], axis=1)
    dq, dk, dv, beta, gc = _delta_prep(proj, proj_ba, cbuf8, wts["conv_w"], wts["alog_pad"], wts["dtb_pad"],
                                       seq_len, tp, chunk)
    qa, kf, ka, va = _attn_prep(proj, wts["gq2"], wts["gk2"], d, _pick(t, 256),
                                col["fq"], col["fk"], col["fv"])

    rows = min(seq_len, 2 * chunk)
    oa, s_new = _delta_rule(dq, dk, dv, proj, beta, gc, s0, wts["delta_norm_g"], seq_len, chunk, rows, col["z"])

    if past_k is None:
        ob = _attn_prompt(qa, ka, va, wts["lam4"], wts["diff_norm_g"], seq_len, _pick(seq_len, 512), lam_init)
    else:
        ob = _attn_cached(qa, ka, va, past_k.reshape(nb, past_k.shape[1], d),
                          past_v.reshape(nb, past_v.shape[1], d), wts["lam4"], wts["diff_norm_g"],
                          seq_len, lam_init)

    tmo = _pick(t, 256)
    h, xnt = _merge_out(x2, oa, ob, proj, wts["w_out"], wts["norm_ffn_g"], tmo, col["ga"], col["gb"])
    s0z, s1, tau = _peer_route(xnt, wts["wqt"], wts["sk"], _pick(t, 256))
    y = _peer_mix(xnt, wts["u"], wts["vt"], s0z, s1, tau, h, _pick(t, 512), 1024)

    new_k = kf.reshape(nb, seq_len, n_heads, 2, DIFF_QK)
    new_v = proj[:, col["fv"] * d:(col["fv"] + 1) * d].reshape(nb, seq_len, n_heads, HEAD_W)
    assert seq_len >= CONV_W - 1
    new_conv = proj[:, :qkv_w].reshape(nb, seq_len, qkv_w)[:, seq_len - (CONV_W - 1):]
    return y.reshape(nb, seq_len, d), new_k, new_v, s_new, new_conv


def _prep_weights(l, d, norm_mix_g, w_in, conv_w, a_log, dt_bias, delta_norm_g, q_norm_g, k_norm_g,
                  lq1, lk1, lq2, lk2, diff_norm_g, w_out, norm_ffn_g, peer_w_q, peer_sub_keys, peer_u, peer_v):
    n_heads = d // HEAD_W
    qkv_w = 3 * d
    w = w_in[l]
    o_z = qkv_w
    o_beta = o_z + d
    o_a = o_beta + n_heads
    o_fq = o_a + n_heads
    w_main = jnp.concatenate([w[:, :o_beta], w[:, o_fq:]], axis=1).astype(BF16)
    w_ba = jnp.pad(w[:, o_beta:o_fq], ((0, 0), (0, LANES - 2 * n_heads))).astype(BF16)

    def lane_pad(v):
        return jnp.pad(v[l].astype(F32), (n_heads, LANES - 2 * n_heads)).reshape(1, LANES)

    return {
        "norm_mix_g": norm_mix_g[l].reshape(1, d),
        "w_main": w_main,
        "w_ba": w_ba,
        "conv_w": conv_w[l],
        "alog_pad": lane_pad(a_log),
        "dtb_pad": lane_pad(dt_bias),
        "delta_norm_g": delta_norm_g[l].reshape(1, HEAD_W),
        "gq2": jnp.tile(q_norm_g[l], 2).reshape(1, HEAD_W),
        "gk2": jnp.tile(k_norm_g[l], 2).reshape(1, HEAD_W),
        "lam4": jnp.stack([lq1[l], lk1[l], lq2[l], lk2[l]]).astype(F32),
        "diff_norm_g": diff_norm_g[l].reshape(1, HEAD_W),
        "w_out": w_out[l].astype(BF16),
        "norm_ffn_g": norm_ffn_g[l].reshape(1, d),
        "wqt": peer_w_q[l].T.astype(BF16),
        "sk": peer_sub_keys[l].astype(BF16),
        "u": peer_u[l].astype(BF16),
        "vt": peer_v[l].T.astype(BF16),
    }


def kernel(x_prompt, x_sample, cache_diff_k, cache_diff_v, state_delta_s, state_delta_conv, norm_mix_g, w_in, conv_w, delta_a_log, delta_dt_bias, delta_norm_g, diff_q_norm_g, diff_k_norm_g, diff_lambda_q1, diff_lambda_k1, diff_lambda_q2, diff_lambda_k2, diff_norm_g, w_out, norm_ffn_g, peer_w_q, peer_sub_keys, peer_u, peer_v):
    depth = w_in.shape[0]
    d = x_prompt.shape[-1]
    n_heads = d // HEAD_W
    yp, ys = x_prompt, x_sample
    outs_p, outs_s = [], []
    for l in range(depth):
        wts = _prep_weights(l, d, norm_mix_g, w_in, conv_w, delta_a_log, delta_dt_bias, delta_norm_g,
                            diff_q_norm_g, diff_k_norm_g, diff_lambda_q1, diff_lambda_k1, diff_lambda_q2,
                            diff_lambda_k2, diff_norm_g, w_out, norm_ffn_g, peer_w_q, peer_sub_keys,
                            peer_u, peer_v)
        lam_init = 0.8 - 0.6 * math.exp(-0.3 * l)
        conv0 = jnp.zeros((yp.shape[0], CONV_W - 1, 3 * d), F32)
        s0 = jnp.zeros((yp.shape[0], n_heads, HEAD_W, HEAD_W), F32)
        yp, *rest = _layer(yp, conv0, s0, None, None, lam_init, wts)
        outs_p.append(rest)
        ys, *rest = _layer(ys, state_delta_conv[l], state_delta_s[l], cache_diff_k[l], cache_diff_v[l],
                           lam_init, wts)
        outs_s.append(rest)
    stack = lambda outs, i: jnp.stack([o[i] for o in outs])
    return (yp, ys,
            stack(outs_p, 0), stack(outs_p, 1), stack(outs_p, 2), stack(outs_p, 3),
            stack(outs_s, 0), stack(outs_s, 1), stack(outs_s, 2), stack(outs_s, 3))
```

```python
import functools
import math

import jax
import jax.numpy as jnp
from jax import lax
from jax.experimental import pallas as pl
from jax.experimental.pallas import tpu as pltpu

F32 = jnp.float32
BF16 = jnp.bfloat16

RMS_EPS = 1e-6
NEG_INF = -1e30
LOG2_E = math.log2(math.e)
CHUNK = 64
HEAD_W = 128
N_HEADS = 8
DIFF_QK = 64
CONV_W = 4
PEER_HEADS = 8
N_KEYS = 128
PEER_TOPK = 16
LANES = 128
SUBLANES = 8
VMEM_LIMIT = 56 * 1024 * 1024


def _cparams(sem):
    return pltpu.CompilerParams(dimension_semantics=sem, vmem_limit_bytes=VMEM_LIMIT)


def _dot(a, b):
    return jnp.dot(a, b, preferred_element_type=F32)


def _dot_nt(a, b):
    return lax.dot_general(a, b, (((1,), (1,)), ((), ())), preferred_element_type=F32)


def _dot_tn(a, b):
    return lax.dot_general(a, b, (((0,), (0,)), ((), ())), preferred_element_type=F32)


def _sigmoid(x):
    return 1.0 / (1.0 + jnp.exp(-x))


def _silu(x):
    return x * _sigmoid(x)


def _in_proj_kernel(x_ref, g_ref, w_ref, wba_ref, o_ref, oba_ref, xn_ref):
    @pl.when(pl.program_id(1) == 0)
    def _():
        x = x_ref[...]
        r = lax.rsqrt(jnp.mean(x * x, axis=-1, keepdims=True) + RMS_EPS)
        xn = (x * r * g_ref[...]).astype(BF16)
        xn_ref[...] = xn
        oba_ref[...] = _dot(xn, wba_ref[...])

    o_ref[...] = _dot(xn_ref[...], w_ref[...])


def _in_proj(x, g, w_main, w_ba, tm, tn):
    t, d = x.shape
    nw = w_main.shape[1]
    return pl.pallas_call(
        _in_proj_kernel,
        grid=(t // tm, nw // tn),
        in_specs=[
            pl.BlockSpec((tm, d), lambda i, j: (i, 0)),
            pl.BlockSpec((1, d), lambda i, j: (0, 0)),
            pl.BlockSpec((d, tn), lambda i, j: (0, j)),
            pl.BlockSpec((d, LANES), lambda i, j: (0, 0)),
        ],
        out_specs=[
            pl.BlockSpec((tm, tn), lambda i, j: (i, j)),
            pl.BlockSpec((tm, LANES), lambda i, j: (i, 0)),
        ],
        out_shape=[jax.ShapeDtypeStruct((t, nw), F32), jax.ShapeDtypeStruct((t, LANES), F32)],
        scratch_shapes=[pltpu.VMEM((tm, d), BF16)],
        compiler_params=_cparams(("parallel", "arbitrary")),
        name="in_proj",
    )(x, g, w_main, w_ba)


def _delta_prep_kernel(cur_ref, prev_ref, cbuf_ref, cw_ref, ba_ref, alog_ref, dtb_ref,
                       q_ref, k_ref, v_ref, beta_ref, gc_ref, *, tiles_per_seq, chunk):
    tm = cur_ref.shape[0]
    qkv_w = cur_ref.shape[1]
    d = qkv_w // 3
    first = (pl.program_id(0) % tiles_per_seq) == 0
    col_w = 4 * HEAD_W
    for c0 in range(0, qkv_w, col_w):
        cur = cur_ref[:, c0:c0 + col_w]
        prev = jnp.where(first, cbuf_ref[:, c0:c0 + col_w], prev_ref[:, c0:c0 + col_w])
        ext = jnp.concatenate([prev, cur], axis=0)
        conv = cw_ref[0:1, c0:c0 + col_w] * ext[SUBLANES - 3:SUBLANES - 3 + tm]
        for j in range(1, CONV_W):
            conv = conv + cw_ref[j:j + 1, c0:c0 + col_w] * ext[SUBLANES - 3 + j:SUBLANES - 3 + j + tm]
        act = _silu(conv)
        for hh in range(col_w // HEAD_W):
            col = c0 + hh * HEAD_W
            a = act[:, hh * HEAD_W:(hh + 1) * HEAD_W]
            if col < 2 * d:
                a = a * lax.rsqrt(jnp.sum(a * a, axis=-1, keepdims=True) + RMS_EPS)
            if col < d:
                q_ref[:, col:col + HEAD_W] = (a * (HEAD_W ** -0.5)).astype(BF16)
            elif col < 2 * d:
                k_ref[:, col - d:col - d + HEAD_W] = a.astype(BF16)
            else:
                v_ref[:, col - 2 * d:col - 2 * d + HEAD_W] = a.astype(BF16)
    ba = ba_ref[...]
    beta_ref[...] = _sigmoid(ba)
    xs = ba + dtb_ref[...]
    softplus = jnp.maximum(xs, 0.0) + jnp.log(1.0 + jnp.exp(-jnp.abs(xs)))
    g = -jnp.exp(alog_ref[...]) * softplus
    row = lax.broadcasted_iota(jnp.int32, (tm, tm), 0)
    colm = lax.broadcasted_iota(jnp.int32, (tm, tm), 1)
    tri = jnp.where((row // chunk == colm // chunk) & (colm <= row), 1.0, 0.0).astype(F32)
    gc_ref[...] = jnp.dot(tri, g, preferred_element_type=F32, precision=lax.Precision.HIGHEST)


def _delta_prep(proj, proj_ba, cbuf8, conv_w, alog_pad, dtb_pad, seq_len, tm, chunk):
    t = proj.shape[0]
    qkv_w = conv_w.shape[1]
    d = qkv_w // 3
    tiles_per_seq = seq_len // tm
    rows8 = tm // SUBLANES
    kern = functools.partial(_delta_prep_kernel, tiles_per_seq=tiles_per_seq, chunk=chunk)
    return pl.pallas_call(
        kern,
        grid=(t // tm,),
        in_specs=[
            pl.BlockSpec((tm, qkv_w), lambda i: (i, 0)),
            pl.BlockSpec((SUBLANES, qkv_w), lambda i: (jnp.maximum(i * rows8 - 1, 0), 0)),
            pl.BlockSpec((None, SUBLANES, qkv_w), lambda i: (i // tiles_per_seq, 0, 0)),
            pl.BlockSpec((CONV_W, qkv_w), lambda i: (0, 0)),
            pl.BlockSpec((tm, LANES), lambda i: (i, 0)),
            pl.BlockSpec((1, LANES), lambda i: (0, 0)),
            pl.BlockSpec((1, LANES), lambda i: (0, 0)),
        ],
        out_specs=[
            pl.BlockSpec((tm, d), lambda i: (i, 0)),
            pl.BlockSpec((tm, d), lambda i: (i, 0)),
            pl.BlockSpec((tm, d), lambda i: (i, 0)),
            pl.BlockSpec((tm, LANES), lambda i: (i, 0)),
            pl.BlockSpec((tm, LANES), lambda i: (i, 0)),
        ],
        out_shape=[jax.ShapeDtypeStruct((t, d), BF16)] * 3 + [jax.ShapeDtypeStruct((t, LANES), F32)] * 2,
        compiler_params=_cparams(("parallel",)),
        name="delta_prep",
    )(proj, proj, cbuf8, conv_w, proj_ba, alog_pad, dtb_pad)


V_AUG = HEAD_W + 16


def _attn_prep_kernel(fq_ref, fk_ref, fv_ref, gq_ref, gk_ref, q_ref, kf_ref, ka_ref, v_ref, *, transposed):
    tm, d = fq_ref.shape
    lane = lax.broadcasted_iota(jnp.int32, (tm, HEAD_W), 1)
    low = lane < DIFF_QK

    def qk_norm(x, g):
        s = x * x
        tot = jnp.sum(s, axis=-1, keepdims=True)
        lo = jnp.sum(jnp.where(low, s, 0.0), axis=-1, keepdims=True)
        ms = jnp.where(low, lo, tot - lo) * (1.0 / DIFF_QK)
        return x * lax.rsqrt(ms + RMS_EPS) * g

    for h in range(d // HEAD_W):
        sl = slice(h * HEAD_W, (h + 1) * HEAD_W)
        qn = qk_norm(fq_ref[:, sl], gq_ref[...]) * (DIFF_QK ** -0.5)
        kn = qk_norm(fk_ref[:, sl], gk_ref[...])
        kf_ref[:, sl] = kn
        ka_ref[:, sl] = kn.astype(BF16)
        if transposed:
            q_ref[sl, :] = qn.T.astype(BF16)
            v_ref[h, 0:HEAD_W, :] = fv_ref[:, sl].T.astype(BF16)
            v_ref[h, HEAD_W:V_AUG, :] = jnp.ones((V_AUG - HEAD_W, tm), BF16)
        else:
            q_ref[:, sl] = qn.astype(BF16)
            v_ref[:, sl] = fv_ref[:, sl].astype(BF16)


def _attn_prep(proj, gq2, gk2, d, tm, col_q, col_k, col_v, seq_len, transposed):
    t = proj.shape[0]
    n_heads = d // HEAD_W
    row = pl.BlockSpec((tm, d), lambda i: (i, 0))
    if transposed:
        per_seq = seq_len // tm
        q_spec = pl.BlockSpec((d, tm), lambda i: (0, i))
        q_shape = jax.ShapeDtypeStruct((d, t), BF16)
        v_spec = pl.BlockSpec((None, n_heads, None, V_AUG, tm), lambda i: (i // per_seq, 0, i % per_seq, 0, 0))
        v_shape = jax.ShapeDtypeStruct((t // seq_len, n_heads, per_seq, V_AUG, tm), BF16)
    else:
        q_spec, q_shape = row, jax.ShapeDtypeStruct((t, d), BF16)
        v_spec, v_shape = row, jax.ShapeDtypeStruct((t, d), BF16)
    return pl.pallas_call(
        functools.partial(_attn_prep_kernel, transposed=transposed),
        grid=(t // tm,),
        in_specs=[
            pl.BlockSpec((tm, d), lambda i: (i, col_q)),
            pl.BlockSpec((tm, d), lambda i: (i, col_k)),
            pl.BlockSpec((tm, d), lambda i: (i, col_v)),
            pl.BlockSpec((1, HEAD_W), lambda i: (0, 0)),
            pl.BlockSpec((1, HEAD_W), lambda i: (0, 0)),
        ],
        out_specs=[q_spec, row, row, v_spec],
        out_shape=[q_shape, jax.ShapeDtypeStruct((t, d), F32), jax.ShapeDtypeStruct((t, d), BF16), v_shape],
        compiler_params=_cparams(("parallel",)),
        name="attn_prep",
    )(proj, proj, proj, gq2, gk2)


def _delta_rule_kernel(q_ref, k_ref, v_ref, z_ref, beta_ref, gc_ref, s0_ref, ng_ref,
                       o_ref, sfin_ref, s_ref, *, chunk, n_chunks):
    n = pl.program_id(1)
    rows = q_ref.shape[0]
    n_heads = q_ref.shape[1] // HEAD_W

    @pl.when(n == 0)
    def _():
        s_ref[...] = s0_ref[...]

    lane = lax.broadcasted_iota(jnp.int32, (rows, LANES), 1)
    ri = lax.broadcasted_iota(jnp.int32, (chunk, chunk), 0)
    ci = lax.broadcasted_iota(jnp.int32, (chunk, chunk), 1)
    eye = ri == ci
    incl = ci <= ri
    strict = ci < ri
    eye_f = jnp.where(eye, 1.0, 0.0).astype(F32)
    beta_all = beta_ref[...]
    gc_all = gc_ref[...]
    n_stage = int(math.log2(chunk))
    heads = range(n_heads)
    chunks = range(rows // chunk)
    items = [(h, c) for c in chunks for h in heads]

    beta_h = [jnp.sum(jnp.where(lane == h, beta_all, 0.0), axis=-1, keepdims=True) for h in heads]
    g_h = [jnp.sum(jnp.where(lane == n_heads + h, gc_all, 0.0), axis=-1, keepdims=True) for h in heads]
    q16, k16, kq, decay, eg, glast, gcol, rhs = {}, {}, {}, {}, {}, {}, {}, {}
    for it in items:
        h, c = it
        sl = slice(h * HEAD_W, (h + 1) * HEAD_W)
        rs = slice(c * chunk, (c + 1) * chunk)
        q16[it] = q_ref[rs, sl]
        k16[it] = k_ref[rs, sl]
        k = k16[it].astype(F32)
        beta = beta_h[h][rs]
        gcol[it] = g_h[h][rs]
        grow = jnp.sum(jnp.where(eye, gcol[it], 0.0), axis=0, keepdims=True)
        decay[it] = jnp.where(incl, jnp.exp(jnp.where(incl, gcol[it] - grow, 0.0)), 0.0)
        eg[it] = jnp.exp(gcol[it])
        glast[it] = gcol[it][chunk - 1:chunk]
        kb = k * beta
        kq[it] = jnp.concatenate([kb.astype(BF16), q16[it]], axis=0)
        rhs[it] = jnp.concatenate([(kb * eg[it]).astype(BF16),
                                   (v_ref[rs, sl].astype(F32) * beta).astype(BF16)], axis=1)
    kk = {it: _dot_nt(kq[it], k16[it]) for it in items}
    mp = {it: -jnp.where(strict, kk[it][0:chunk] * decay[it], 0.0) for it in items}
    tmat = {it: eye_f + mp[it] for it in items}
    for stage in range(1, n_stage):
        m16 = {it: mp[it].astype(BF16) for it in items}
        mp = {it: _dot(m16[it], m16[it]) for it in items}
        m16 = {it: mp[it].astype(BF16) for it in items}
        tmat = {it: tmat[it] + _dot(m16[it], tmat[it].astype(BF16)) for it in items}
    wu = {it: _dot(tmat[it].astype(BF16), rhs[it]) for it in items}
    intra16 = {it: (kk[it][chunk:2 * chunk] * decay[it]).astype(BF16) for it in items}

    s = [s_ref[h] for h in heads]
    for c in chunks:
        s16 = [s[h].astype(BF16) for h in heads]
        lhs = [jnp.concatenate([wu[(h, c)][:, 0:HEAD_W].astype(BF16),
                                (q16[(h, c)].astype(F32) * eg[(h, c)]).astype(BF16)], axis=0) for h in heads]
        ws = [_dot(lhs[h], s16[h]) for h in heads]
        vn16 = [(wu[(h, c)][:, HEAD_W:2 * HEAD_W] - ws[h][0:chunk]).astype(BF16) for h in heads]
        o = [ws[h][chunk:2 * chunk] + _dot(intra16[(h, c)], vn16[h]) for h in heads]
        kd16 = [(k16[(h, c)].astype(F32) * jnp.exp(glast[(h, c)] - gcol[(h, c)])).astype(BF16) for h in heads]
        s = [s[h] * jnp.exp(glast[(h, c)]) + _dot_tn(kd16[h], vn16[h]) for h in heads]
        for h in heads:
            sl = slice(h * HEAD_W, (h + 1) * HEAD_W)
            rs = slice(c * chunk, (c + 1) * chunk)
            on = o[h] * lax.rsqrt(jnp.mean(o[h] * o[h], axis=-1, keepdims=True) + RMS_EPS) * ng_ref[...]
            o_ref[rs, sl] = (on * _silu(z_ref[rs, sl])).astype(BF16)
    for h in heads:
        s_ref[h] = s[h]

    @pl.when(n == n_chunks - 1)
    def _():
        sfin_ref[...] = s_ref[...]


def _delta_rule(dq, dk, dv, proj, beta, gc, s0, ng, seq_len, chunk, rows, col_z):
    t, d = dq.shape
    nb = t // seq_len
    n_steps = seq_len // rows
    n_heads = d // HEAD_W
    kern = functools.partial(_delta_rule_kernel, chunk=chunk, n_chunks=n_steps)
    tok = lambda b, n: (b * n_steps + n, 0)
    return pl.pallas_call(
        kern,
        grid=(nb, n_steps),
        in_specs=[
            pl.BlockSpec((rows, d), tok),
            pl.BlockSpec((rows, d), tok),
            pl.BlockSpec((rows, d), tok),
            pl.BlockSpec((rows, d), lambda b, n: (b * n_steps + n, col_z)),
            pl.BlockSpec((rows, LANES), tok),
            pl.BlockSpec((rows, LANES), tok),
            pl.BlockSpec((None, n_heads, HEAD_W, HEAD_W), lambda b, n: (b, 0, 0, 0)),
            pl.BlockSpec((1, HEAD_W), lambda b, n: (0, 0)),
        ],
        out_specs=[
            pl.BlockSpec((rows, d), tok),
            pl.BlockSpec((None, n_heads, HEAD_W, HEAD_W), lambda b, n: (b, 0, 0, 0)),
        ],
        out_shape=[jax.ShapeDtypeStruct((t, d), BF16),
                   jax.ShapeDtypeStruct((nb, n_heads, HEAD_W, HEAD_W), F32)],
        scratch_shapes=[pltpu.VMEM((n_heads, HEAD_W, HEAD_W), F32)],
        compiler_params=_cparams(("parallel", "arbitrary")),
        name="delta_rule",
    )(dq, dk, dv, proj, beta, gc, s0, ng)


def _lambda(lam_ref, lam_init):
    l = lam_ref[...]
    a = jnp.sum(l[0:1] * l[1:2], axis=-1, keepdims=True)
    b = jnp.sum(l[2:3] * l[3:4], axis=-1, keepdims=True)
    return jnp.exp(a) - jnp.exp(b) + lam_init


def _sub_norm(o, g, lam_init):
    return o * lax.rsqrt(jnp.mean(o * o, axis=-1, keepdims=True) + RMS_EPS) * g * (1.0 - lam_init)


def _attn_prompt_kernel(qt_ref, k_ref, vt_ref, lam_ref, ng_ref, o_ref, m_ref, acc_ref, sa_ref, sb_ref, *, lam_init):
    i = pl.program_id(2)
    bq = qt_ref.shape[1]
    bk = vt_ref.shape[-1]
    assert bq == 2 * bk
    qt = qt_ref[...]
    row = lax.broadcasted_iota(jnp.int32, qt.shape, 0)
    zero = jnp.zeros_like(qt)
    qm = (jnp.where(row < DIFF_QK, qt, zero), jnp.where(row >= DIFF_QK, qt, zero))
    m_ref[...] = jnp.full(m_ref.shape, NEG_INF, F32)
    acc_ref[...] = jnp.zeros(acc_ref.shape, F32)

    def scores(j, s_ref):
        kb = k_ref[pl.ds(pl.multiple_of(j * bk, bk), bk), :]
        for m in range(2):
            s_ref[m] = _dot(kb, qm[m])

    def absorb(j, s_ref, mask):
        vb = vt_ref[j]
        for m in range(2):
            s = s_ref[m]
            if mask is not None:
                s = jnp.where(mask, s, NEG_INF)
            m_prev = m_ref[m]
            m_new = jnp.maximum(m_prev, jnp.max(s, axis=0, keepdims=True))
            alpha = jnp.exp(m_prev - m_new)
            p = jnp.exp(s - m_new).astype(BF16)
            acc_ref[m] = alpha * acc_ref[m] + _dot(vb, p)
            m_ref[m] = m_new

    scores(0, sa_ref)

    def body(jj, carry):
        scores(2 * jj + 1, sb_ref)
        absorb(2 * jj, sa_ref, None)
        scores(2 * jj + 2, sa_ref)
        absorb(2 * jj + 1, sb_ref, None)
        return carry

    lax.fori_loop(0, i, body, 0)
    scores(2 * i + 1, sb_ref)
    ri = lax.broadcasted_iota(jnp.int32, (bk, bq), 0)
    ci = lax.broadcasted_iota(jnp.int32, (bk, bq), 1)
    absorb(2 * i, sa_ref, (ri // CHUNK) <= (ci // CHUNK))
    absorb(2 * i + 1, sb_ref, ((ri + bk) // CHUNK) <= (ci // CHUNK))

    lam = _lambda(lam_ref, lam_init)
    a0 = acc_ref[0]
    a1 = acc_ref[1]
    ot = a0[0:HEAD_W] / a0[HEAD_W:HEAD_W + 1] - lam * (a1[0:HEAD_W] / a1[HEAD_W:HEAD_W + 1])
    o_ref[...] = _sub_norm(ot.T, ng_ref[...], lam_init).astype(BF16)


def _attn_prompt(qt, ka, vt, lam4, ng, seq_len, bk, lam_init):
    d, t = qt.shape
    nb = t // seq_len
    bq = 2 * bk
    nq = seq_len // bq
    nk = seq_len // bk
    n_heads = d // HEAD_W
    kern = functools.partial(_attn_prompt_kernel, lam_init=lam_init)
    return pl.pallas_call(
        kern,
        grid=(nb, n_heads, nq),
        in_specs=[
            pl.BlockSpec((HEAD_W, bq), lambda b, h, i: (h, b * nq + i)),
            pl.BlockSpec((seq_len, HEAD_W), lambda b, h, i: (b, h)),
            pl.BlockSpec((None, None, nk, V_AUG, bk), lambda b, h, i: (b, h, 0, 0, 0)),
            pl.BlockSpec((4, DIFF_QK), lambda b, h, i: (0, 0)),
            pl.BlockSpec((1, HEAD_W), lambda b, h, i: (0, 0)),
        ],
        out_specs=pl.BlockSpec((bq, HEAD_W), lambda b, h, i: (b * nq + i, h)),
        out_shape=jax.ShapeDtypeStruct((t, d), BF16),
        scratch_shapes=[pltpu.VMEM((2, 1, bq), F32), pltpu.VMEM((2, V_AUG, bq), F32),
                        pltpu.VMEM((2, bk, bq), F32), pltpu.VMEM((2, bk, bq), F32)],
        compiler_params=_cparams(("parallel", "parallel", "arbitrary")),
        name="attn_prompt",
    )(qt, ka, vt, lam4, ng)


def _attn_cached_kernel(q_ref, kn_ref, vn_ref, ck_ref, cv_ref, lam_ref, ng_ref, o_ref, *, lam_init):
    q = q_ref[...]
    lane = lax.broadcasted_iota(jnp.int32, q.shape, 1)
    zero = jnp.zeros_like(q)
    kp = ck_ref[...].astype(BF16)
    vp = cv_ref[...].astype(BF16)
    kn = kn_ref[...]
    vn = vn_ref[...]
    outs = []
    for m in range(2):
        qm = jnp.where((lane < DIFF_QK) if m == 0 else (lane >= DIFF_QK), q, zero)
        sp = _dot_nt(qm, kp)
        sn = _dot_nt(qm, kn)
        mx = jnp.maximum(jnp.max(sp, axis=-1, keepdims=True), jnp.max(sn, axis=-1, keepdims=True))
        pp = jnp.exp(sp - mx)
        pn = jnp.exp(sn - mx)
        den = jnp.sum(pp, axis=-1, keepdims=True) + jnp.sum(pn, axis=-1, keepdims=True)
        outs.append((_dot(pp.astype(BF16), vp) + _dot(pn.astype(BF16), vn)) / den)
    lam = _lambda(lam_ref, lam_init)
    o = outs[0] - lam * outs[1]
    o_ref[...] = _sub_norm(o, ng_ref[...], lam_init).astype(BF16)


def _attn_cached(qa, ka, va, cache_k, cache_v, lam4, ng, seq_len, lam_init):
    t, d = qa.shape
    nb = t // seq_len
    past = cache_k.shape[1]
    n_heads = d // HEAD_W
    kern = functools.partial(_attn_cached_kernel, lam_init=lam_init)
    tok = lambda b, h: (b, h)
    return pl.pallas_call(
        kern,
        grid=(nb, n_heads),
        in_specs=[
            pl.BlockSpec((seq_len, HEAD_W), tok),
            pl.BlockSpec((seq_len, HEAD_W), tok),
            pl.BlockSpec((seq_len, HEAD_W), tok),
            pl.BlockSpec((None, past, HEAD_W), lambda b, h: (b, 0, h)),
            pl.BlockSpec((None, past, HEAD_W), lambda b, h: (b, 0, h)),
            pl.BlockSpec((4, DIFF_QK), lambda b, h: (0, 0)),
            pl.BlockSpec((1, HEAD_W), lambda b, h: (0, 0)),
        ],
        out_specs=pl.BlockSpec((seq_len, HEAD_W), tok),
        out_shape=jax.ShapeDtypeStruct((t, d), BF16),
        compiler_params=_cparams(("parallel", "parallel")),
        name="attn_cached",
    )(qa, ka, va, cache_k, cache_v, lam4, ng)


def _merge_out_kernel(x_ref, oa_ref, ob_ref, ga_ref, gb_ref, wo_ref, g2_ref, h_ref, xnt_ref):
    merged = (_sigmoid(ga_ref[...]) * oa_ref[...].astype(F32)
              + _sigmoid(gb_ref[...]) * ob_ref[...].astype(F32))
    h = x_ref[...] + _dot(merged.astype(BF16), wo_ref[...])
    h_ref[...] = h
    hn = h * lax.rsqrt(jnp.mean(h * h, axis=-1, keepdims=True) + RMS_EPS) * g2_ref[...]
    xnt_ref[...] = hn.T.astype(BF16)


def _merge_out(x, oa, ob, proj, w_out, g2, tm, col_ga, col_gb):
    t, d = x.shape
    row = lambda i: (i, 0)
    return pl.pallas_call(
        _merge_out_kernel,
        grid=(t // tm,),
        in_specs=[
            pl.BlockSpec((tm, d), row),
            pl.BlockSpec((tm, d), row),
            pl.BlockSpec((tm, d), row),
            pl.BlockSpec((tm, d), lambda i: (i, col_ga)),
            pl.BlockSpec((tm, d), lambda i: (i, col_gb)),
            pl.BlockSpec((d, d), lambda i: (0, 0)),
            pl.BlockSpec((1, d), lambda i: (0, 0)),
        ],
        out_specs=[pl.BlockSpec((tm, d), row), pl.BlockSpec((d, tm), lambda i: (0, i))],
        out_shape=[jax.ShapeDtypeStruct((t, d), F32), jax.ShapeDtypeStruct((d, t), BF16)],
        compiler_params=_cparams(("parallel",)),
        name="merge_out",
    )(x, oa, ob, proj, proj, w_out, g2)


def _top_values(parts, count):
    vals = []
    for r in range(count):
        m = parts[0]
        for p in parts[1:]:
            m = jnp.maximum(m, p)
        m = jnp.max(m, axis=0, keepdims=True)
        vals.append(m)
        if r + 1 < count:
            parts = [jnp.where(p == m, NEG_INF, p) for p in parts]
    return vals


def _peer_route_kernel(xnt_ref, wqt_ref, sk_ref, s0_ref, s1_ref, tau_ref):
    pq = _dot(wqt_ref[...], xnt_ref[...]).astype(BF16)
    dh = sk_ref.shape[-1]
    taus = []
    for h in range(PEER_HEADS):
        sc = []
        for p in range(2):
            r0 = (h * 2 + p) * dh
            sc.append(_dot(sk_ref[h, p], pq[r0:r0 + dh]))
        sv0 = _top_values([sc[0]], PEER_TOPK)
        sv1 = jnp.concatenate(_top_values([sc[1]], PEER_TOPK), axis=0)
        cand = [sv0[i] + sv1 for i in range(PEER_TOPK)]
        tv = _top_values(cand, PEER_TOPK + 1)
        mx = tv[0]
        z = jnp.exp(tv[0] - mx)
        for r in range(1, PEER_TOPK):
            z = z + jnp.exp(tv[r] - mx)
        logz = jnp.log(z)
        taus.append((0.5 * (tv[PEER_TOPK - 1] + tv[PEER_TOPK]) - mx - logz) * LOG2_E)
        s0_ref[h] = (sc[0] - sv0[0] - logz) * LOG2_E
        s1_ref[h] = (sc[1] - sv1[0:1]) * LOG2_E
    tau_ref[...] = jnp.concatenate(taus, axis=0)


def _peer_route(xnt, wqt, sk16, tm):
    d, t = xnt.shape
    qw = wqt.shape[0]
    tok = lambda i: (0, 0, i)
    return pl.pallas_call(
        _peer_route_kernel,
        grid=(t // tm,),
        in_specs=[
            pl.BlockSpec((d, tm), lambda i: (0, i)),
            pl.BlockSpec((qw, d), lambda i: (0, 0)),
            pl.BlockSpec(sk16.shape, lambda i: (0, 0, 0, 0)),
        ],
        out_specs=[
            pl.BlockSpec((PEER_HEADS, N_KEYS, tm), tok),
            pl.BlockSpec((PEER_HEADS, N_KEYS, tm), tok),
            pl.BlockSpec((PEER_HEADS, tm), lambda i: (0, i)),
        ],
        out_shape=[jax.ShapeDtypeStruct((PEER_HEADS, N_KEYS, t), F32),
                   jax.ShapeDtypeStruct((PEER_HEADS, N_KEYS, t), F32),
                   jax.ShapeDtypeStruct((PEER_HEADS, t), F32)],
        compiler_params=_cparams(("parallel",)),
        name="peer_route",
    )(xnt, wqt, sk16)


def _gelu(x):
    return 0.5 * x * (1.0 + lax.erf(x * (2.0 ** -0.5)))


GROUP_ROWS = 64


def _peer_mix_kernel(xnt_ref, u_ref, vt_ref, s0_ref, s1_ref, tau_ref, h_ref, y_ref, acc_ref, e1_ref, gw_ref,
                     *, n_eblk):
    j = pl.program_id(1)
    na = s0_ref.shape[1]
    tm = xnt_ref.shape[1]

    @pl.when(j == 0)
    def _():
        acc_ref[...] = jnp.zeros(acc_ref.shape, F32)
        gw_ref[...] = jnp.zeros(gw_ref.shape, BF16)
        e1_ref[...] = jnp.exp2(s1_ref[...])

    xnt = xnt_ref[...]
    gw_prev = gw_ref[(j + 1) % 2]
    d_rows = acc_ref.shape[0] // na
    def activations(a):
        return _dot(u_ref[a * N_KEYS:(a + 1) * N_KEYS, :], xnt)

    slabs = []
    act_next = activations(0)
    for a in range(na):
        act = act_next
        if a + 1 < na:
            act_next = activations(a + 1)
        ds = slice(a * d_rows, (a + 1) * d_rows)
        acc_ref[ds, :] += _dot(vt_ref[ds, :], gw_prev)
        for b0 in range(0, N_KEYS, GROUP_ROWS):
            bs = slice(b0, b0 + GROUP_ROWS)
            cols = []
            for t0 in range(0, tm, min(tm, LANES)):
                ts = slice(t0, t0 + min(tm, LANES))
                wgt = jnp.zeros((GROUP_ROWS, min(tm, LANES)), F32)
                for h in range(PEER_HEADS):
                    s0 = s0_ref[h, a:a + 1, ts]
                    e1 = e1_ref[h, bs, ts]
                    e1_min = jnp.exp2(tau_ref[h:h + 1, ts] - s0)
                    wgt = wgt + jnp.where(e1 >= e1_min, e1, 0.0) * jnp.exp2(s0)
                cols.append((wgt * _gelu(act[bs, ts])).astype(BF16))
            slabs.append(jnp.concatenate(cols, axis=1))
    gw_ref[j % 2] = jnp.concatenate(slabs, axis=0)

    @pl.when(j == n_eblk)
    def _():
        y_ref[...] = h_ref[...] + acc_ref[...].T


def _peer_mix(xnt, u16, vt16, s0z, s1, tau, h, tm, te):
    d, t = xnt.shape
    n_exp = u16.shape[0]
    na = te // N_KEYS
    n_eblk = n_exp // te
    kern = functools.partial(_peer_mix_kernel, n_eblk=n_eblk)
    cur = lambda j: jnp.minimum(j, n_eblk - 1)
    return pl.pallas_call(
        kern,
        grid=(t // tm, n_eblk + 1),
        in_specs=[
            pl.BlockSpec((d, tm), lambda i, j: (0, i)),
            pl.BlockSpec((te, d), lambda i, j: (cur(j), 0)),
            pl.BlockSpec((d, te), lambda i, j: (0, jnp.maximum(j - 1, 0))),
            pl.BlockSpec((PEER_HEADS, na, tm), lambda i, j: (0, cur(j), i)),
            pl.BlockSpec((PEER_HEADS, N_KEYS, tm), lambda i, j: (0, 0, i)),
            pl.BlockSpec((PEER_HEADS, tm), lambda i, j: (0, i)),
            pl.BlockSpec((tm, d), lambda i, j: (i, 0)),
        ],
        out_specs=pl.BlockSpec((tm, d), lambda i, j: (i, 0)),
        out_shape=jax.ShapeDtypeStruct((t, d), F32),
        scratch_shapes=[pltpu.VMEM((d, tm), F32), pltpu.VMEM((PEER_HEADS, N_KEYS, tm), F32),
                        pltpu.VMEM((2, te, tm), BF16)],
        compiler_params=_cparams(("parallel", "arbitrary")),
        name="peer_mix",
    )(xnt, u16, vt16, s0z, s1, tau, h)


def _pick(n, pref):
    t = min(n, pref)
    while n % t:
        t //= 2
    return t


def _layer(x, conv_buf, s0, past_k, past_v, lam_init, wts):
    nb, seq_len, d = x.shape
    t = nb * seq_len
    n_heads = d // HEAD_W
    qkv_w = 3 * d
    x2 = x.reshape(t, d)

    tm = _pick(t, 512)
    proj, proj_ba = _in_proj(x2, wts["norm_mix_g"], wts["w_main"], wts["w_ba"], tm, 1024)
    col ={name: idx for idx, name in enumerate(("q", "k", "v", "z", "fq", "fk", "fv", "ga", "gb"))}

    chunk = min(CHUNK, seq_len)
    tp = _pick(seq_len, 256)
    cbuf8 = jnp.concatenate(
        [jnp.zeros((nb, SUBLANES - (CONV_W - 1), qkv_w), F32), conv_buf.astype(F32)], axis=1)
    dq, dk, dv, beta, gc = _delta_prep(proj, proj_ba, cbuf8, wts["conv_w"], wts["alog_pad"], wts["dtb_pad"],
                                       seq_len, tp, chunk)
    prompt = past_k is None
    bk = _pick(seq_len // 2, 512)
    qa, kf, ka, va = _attn_prep(proj, wts["gq2"], wts["gk2"], d, bk if prompt else _pick(t, 256),
                                col["fq"], col["fk"], col["fv"], seq_len, prompt)

    rows = min(seq_len, 2 * chunk)
    oa, s_new = _delta_rule(dq, dk, dv, proj, beta, gc, s0, wts["delta_norm_g"], seq_len, chunk, rows, col["z"])

    if prompt:
        ob = _attn_prompt(qa, ka, va, wts["lam4"], wts["diff_norm_g"], seq_len, bk, lam_init)
    else:
        ob = _attn_cached(qa, ka, va, past_k.reshape(nb, past_k.shape[1], d),
                          past_v.reshape(nb, past_v.shape[1], d), wts["lam4"], wts["diff_norm_g"],
                          seq_len, lam_init)

    tmo = _pick(t, 256)
    h, xnt = _merge_out(x2, oa, ob, proj, wts["w_out"], wts["norm_ffn_g"], tmo, col["ga"], col["gb"])
    s0z, s1, tau = _peer_route(xnt, wts["wqt"], wts["sk"], _pick(t, 256))
    y = _peer_mix(xnt, wts["u"], wts["vt"], s0z, s1, tau, h, _pick(t, 512), 1024)

    new_k = kf.reshape(nb, seq_len, n_heads, 2, DIFF_QK)
    new_v = proj[:, col["fv"] * d:(col["fv"] + 1) * d].reshape(nb, seq_len, n_heads, HEAD_W)
    assert seq_len >= CONV_W - 1
    new_conv = proj[:, :qkv_w].reshape(nb, seq_len, qkv_w)[:, seq_len - (CONV_W - 1):]
    return y.reshape(nb, seq_len, d), new_k, new_v, s_new, new_conv


def _prep_weights(l, d, norm_mix_g, w_in, conv_w, a_log, dt_bias, delta_norm_g, q_norm_g, k_norm_g,
                  lq1, lk1, lq2, lk2, diff_norm_g, w_out, norm_ffn_g, peer_w_q, peer_sub_keys, peer_u, peer_v):
    n_heads = d // HEAD_W
    qkv_w = 3 * d
    w = w_in[l]
    o_z = qkv_w
    o_beta = o_z + d
    o_a = o_beta + n_heads
    o_fq = o_a + n_heads
    w_main = jnp.concatenate([w[:, :o_beta], w[:, o_fq:]], axis=1).astype(BF16)
    w_ba = jnp.pad(w[:, o_beta:o_fq], ((0, 0), (0, LANES - 2 * n_heads))).astype(BF16)

    def lane_pad(v):
        return jnp.pad(v[l].astype(F32), (n_heads, LANES - 2 * n_heads)).reshape(1, LANES)

    return {
        "norm_mix_g": norm_mix_g[l].reshape(1, d),
        "w_main": w_main,
        "w_ba": w_ba,
        "conv_w": conv_w[l],
        "alog_pad": lane_pad(a_log),
        "dtb_pad": lane_pad(dt_bias),
        "delta_norm_g": delta_norm_g[l].reshape(1, HEAD_W),
        "gq2": jnp.tile(q_norm_g[l], 2).reshape(1, HEAD_W),
        "gk2": jnp.tile(k_norm_g[l], 2).reshape(1, HEAD_W),
        "lam4": jnp.stack([lq1[l], lk1[l], lq2[l], lk2[l]]).astype(F32),
        "diff_norm_g": diff_norm_g[l].reshape(1, HEAD_W),
        "w_out": w_out[l].astype(BF16),
        "norm_ffn_g": norm_ffn_g[l].reshape(1, d),
        "wqt": peer_w_q[l].T.astype(BF16),
        "sk": peer_sub_keys[l].astype(BF16),
        "u": peer_u[l].astype(BF16),
        "vt": peer_v[l].T.astype(BF16),
    }


def kernel(x_prompt, x_sample, cache_diff_k, cache_diff_v, state_delta_s, state_delta_conv, norm_mix_g, w_in, conv_w, delta_a_log, delta_dt_bias, delta_norm_g, diff_q_norm_g, diff_k_norm_g, diff_lambda_q1, diff_lambda_k1, diff_lambda_q2, diff_lambda_k2, diff_norm_g, w_out, norm_ffn_g, peer_w_q, peer_sub_keys, peer_u, peer_v):
    depth = w_in.shape[0]
    d = x_prompt.shape[-1]
    n_heads = d // HEAD_W
    yp, ys = x_prompt, x_sample
    outs_p, outs_s = [], []
    for l in range(depth):
        wts = _prep_weights(l, d, norm_mix_g, w_in, conv_w, delta_a_log, delta_dt_bias, delta_norm_g,
                            diff_q_norm_g, diff_k_norm_g, diff_lambda_q1, diff_lambda_k1, diff_lambda_q2,
                            diff_lambda_k2, diff_norm_g, w_out, norm_ffn_g, peer_w_q, peer_sub_keys,
                            peer_u, peer_v)
        lam_init = 0.8 - 0.6 * math.exp(-0.3 * l)
        conv0 = jnp.zeros((yp.shape[0], CONV_W - 1, 3 * d), F32)
        s0 = jnp.zeros((yp.shape[0], n_heads, HEAD_W, HEAD_W), F32)
        yp, *rest = _layer(yp, conv0, s0, None, None, lam_init, wts)
        outs_p.append(rest)
        ys, *rest = _layer(ys, state_delta_conv[l], state_delta_s[l], cache_diff_k[l], cache_diff_v[l],
                           lam_init, wts)
        outs_s.append(rest)
    stack = lambda outs, i: jnp.stack([o[i] for o in outs])
    return (yp, ys,
            stack(outs_p, 0), stack(outs_p, 1), stack(outs_p, 2), stack(outs_p, 3),
            stack(outs_s, 0), stack(outs_s, 1), stack(outs_s, 2), stack(outs_s, 3))
```

```python
import functools
import math

import jax
import jax.numpy as jnp
from jax import lax
from jax.experimental import pallas as pl
from jax.experimental.pallas import tpu as pltpu

F32 = jnp.float32
BF16 = jnp.bfloat16

RMS_EPS = 1e-6
NEG_INF = -1e30
LOG2_E = math.log2(math.e)
CHUNK = 64
HEAD_W = 128
N_HEADS = 8
DIFF_QK = 64
CONV_W = 4
PEER_HEADS = 8
N_KEYS = 128
PEER_TOPK = 16
LANES = 128
SUBLANES = 8
VMEM_LIMIT = 56 * 1024 * 1024


def _cparams(sem):
    return pltpu.CompilerParams(dimension_semantics=sem, vmem_limit_bytes=VMEM_LIMIT)


def _dot(a, b):
    return jnp.dot(a, b, preferred_element_type=F32)


def _dot_nt(a, b):
    return lax.dot_general(a, b, (((1,), (1,)), ((), ())), preferred_element_type=F32)


def _dot_tn(a, b):
    return lax.dot_general(a, b, (((0,), (0,)), ((), ())), preferred_element_type=F32)


def _sigmoid(x):
    return 1.0 / (1.0 + jnp.exp(-x))


def _silu(x):
    return x * _sigmoid(x)


def _in_proj_kernel(x_ref, g_ref, w_ref, wba_ref, o_ref, oba_ref, xn_ref):
    @pl.when(pl.program_id(1) == 0)
    def _():
        x = x_ref[...]
        r = lax.rsqrt(jnp.mean(x * x, axis=-1, keepdims=True) + RMS_EPS)
        xn = (x * r * g_ref[...]).astype(BF16)
        xn_ref[...] = xn
        oba_ref[...] = _dot(xn, wba_ref[...])

    o_ref[...] = _dot(xn_ref[...], w_ref[...])


def _in_proj(x, g, w_main, w_ba, tm, tn):
    t, d = x.shape
    nw = w_main.shape[1]
    return pl.pallas_call(
        _in_proj_kernel,
        grid=(t // tm, nw // tn),
        in_specs=[
            pl.BlockSpec((tm, d), lambda i, j: (i, 0)),
            pl.BlockSpec((1, d), lambda i, j: (0, 0)),
            pl.BlockSpec((d, tn), lambda i, j: (0, j)),
            pl.BlockSpec((d, LANES), lambda i, j: (0, 0)),
        ],
        out_specs=[
            pl.BlockSpec((tm, tn), lambda i, j: (i, j)),
            pl.BlockSpec((tm, LANES), lambda i, j: (i, 0)),
        ],
        out_shape=[jax.ShapeDtypeStruct((t, nw), F32), jax.ShapeDtypeStruct((t, LANES), F32)],
        scratch_shapes=[pltpu.VMEM((tm, d), BF16)],
        compiler_params=_cparams(("parallel", "arbitrary")),
        name="in_proj",
    )(x, g, w_main, w_ba)


def _delta_prep_kernel(cur_ref, prev_ref, cbuf_ref, cw_ref, ba_ref, alog_ref, dtb_ref,
                       q_ref, k_ref, v_ref, beta_ref, gc_ref, *, tiles_per_seq, chunk):
    tm = cur_ref.shape[0]
    qkv_w = cur_ref.shape[1]
    d = qkv_w // 3
    first = (pl.program_id(0) % tiles_per_seq) == 0
    col_w = 4 * HEAD_W
    for c0 in range(0, qkv_w, col_w):
        cur = cur_ref[:, c0:c0 + col_w]
        prev = jnp.where(first, cbuf_ref[:, c0:c0 + col_w], prev_ref[:, c0:c0 + col_w])
        ext = jnp.concatenate([prev, cur], axis=0)
        conv = cw_ref[0:1, c0:c0 + col_w] * ext[SUBLANES - 3:SUBLANES - 3 + tm]
        for j in range(1, CONV_W):
            conv = conv + cw_ref[j:j + 1, c0:c0 + col_w] * ext[SUBLANES - 3 + j:SUBLANES - 3 + j + tm]
        act = _silu(conv)
        for hh in range(col_w // HEAD_W):
            col = c0 + hh * HEAD_W
            a = act[:, hh * HEAD_W:(hh + 1) * HEAD_W]
            if col < 2 * d:
                a = a * lax.rsqrt(jnp.sum(a * a, axis=-1, keepdims=True) + RMS_EPS)
            if col < d:
                q_ref[:, col:col + HEAD_W] = (a * (HEAD_W ** -0.5)).astype(BF16)
            elif col < 2 * d:
                k_ref[:, col - d:col - d + HEAD_W] = a.astype(BF16)
            else:
                v_ref[:, col - 2 * d:col - 2 * d + HEAD_W] = a.astype(BF16)
    ba = ba_ref[...]
    beta_ref[...] = _sigmoid(ba)
    xs = ba + dtb_ref[...]
    softplus = jnp.maximum(xs, 0.0) + jnp.log(1.0 + jnp.exp(-jnp.abs(xs)))
    g = -jnp.exp(alog_ref[...]) * softplus
    row = lax.broadcasted_iota(jnp.int32, (tm, tm), 0)
    colm = lax.broadcasted_iota(jnp.int32, (tm, tm), 1)
    tri = jnp.where((row // chunk == colm // chunk) & (colm <= row), 1.0, 0.0).astype(F32)
    gc_ref[...] = jnp.dot(tri, g, preferred_element_type=F32, precision=lax.Precision.HIGHEST)


def _delta_prep(proj, proj_ba, cbuf8, conv_w, alog_pad, dtb_pad, seq_len, tm, chunk):
    t = proj.shape[0]
    qkv_w = conv_w.shape[1]
    d = qkv_w // 3
    tiles_per_seq = seq_len // tm
    rows8 = tm // SUBLANES
    kern = functools.partial(_delta_prep_kernel, tiles_per_seq=tiles_per_seq, chunk=chunk)
    return pl.pallas_call(
        kern,
        grid=(t // tm,),
        in_specs=[
            pl.BlockSpec((tm, qkv_w), lambda i: (i, 0)),
            pl.BlockSpec((SUBLANES, qkv_w), lambda i: (jnp.maximum(i * rows8 - 1, 0), 0)),
            pl.BlockSpec((None, SUBLANES, qkv_w), lambda i: (i // tiles_per_seq, 0, 0)),
            pl.BlockSpec((CONV_W, qkv_w), lambda i: (0, 0)),
            pl.BlockSpec((tm, LANES), lambda i: (i, 0)),
            pl.BlockSpec((1, LANES), lambda i: (0, 0)),
            pl.BlockSpec((1, LANES), lambda i: (0, 0)),
        ],
        out_specs=[
            pl.BlockSpec((tm, d), lambda i: (i, 0)),
            pl.BlockSpec((tm, d), lambda i: (i, 0)),
            pl.BlockSpec((tm, d), lambda i: (i, 0)),
            pl.BlockSpec((tm, LANES), lambda i: (i, 0)),
            pl.BlockSpec((tm, LANES), lambda i: (i, 0)),
        ],
        out_shape=[jax.ShapeDtypeStruct((t, d), BF16)] * 3 + [jax.ShapeDtypeStruct((t, LANES), F32)] * 2,
        compiler_params=_cparams(("parallel",)),
        name="delta_prep",
    )(proj, proj, cbuf8, conv_w, proj_ba, alog_pad, dtb_pad)


V_AUG = HEAD_W + 16


def _attn_prep_kernel(fq_ref, fk_ref, fv_ref, gq_ref, gk_ref, q_ref, kf_ref, ka_ref, v_ref, vf_ref, *, transposed):
    tm, d = fq_ref.shape
    vf_ref[...] = fv_ref[...]
    lane = lax.broadcasted_iota(jnp.int32, (tm, HEAD_W), 1)
    low = lane < DIFF_QK

    def qk_norm(x, g):
        s = x * x
        tot = jnp.sum(s, axis=-1, keepdims=True)
        lo = jnp.sum(jnp.where(low, s, 0.0), axis=-1, keepdims=True)
        ms = jnp.where(low, lo, tot - lo) * (1.0 / DIFF_QK)
        return x * lax.rsqrt(ms + RMS_EPS) * g

    for h in range(d // HEAD_W):
        sl = slice(h * HEAD_W, (h + 1) * HEAD_W)
        qn = qk_norm(fq_ref[:, sl], gq_ref[...]) * (DIFF_QK ** -0.5)
        kn = qk_norm(fk_ref[:, sl], gk_ref[...])
        kf_ref[:, sl] = kn
        ka_ref[:, sl] = kn.astype(BF16)
        if transposed:
            q_ref[sl, :] = qn.T.astype(BF16)
            v_ref[h, 0:HEAD_W, :] = fv_ref[:, sl].T.astype(BF16)
            v_ref[h, HEAD_W:V_AUG, :] = jnp.ones((V_AUG - HEAD_W, tm), BF16)
        else:
            q_ref[:, sl] = qn.astype(BF16)
            v_ref[:, sl] = fv_ref[:, sl].astype(BF16)


def _attn_prep(proj, gq2, gk2, d, tm, col_q, col_k, col_v, seq_len, transposed):
    t = proj.shape[0]
    n_heads = d // HEAD_W
    row = pl.BlockSpec((tm, d), lambda i: (i, 0))
    if transposed:
        per_seq = seq_len // tm
        q_spec = pl.BlockSpec((d, tm), lambda i: (0, i))
        q_shape = jax.ShapeDtypeStruct((d, t), BF16)
        v_spec = pl.BlockSpec((None, n_heads, None, V_AUG, tm), lambda i: (i // per_seq, 0, i % per_seq, 0, 0))
        v_shape = jax.ShapeDtypeStruct((t // seq_len, n_heads, per_seq, V_AUG, tm), BF16)
    else:
        q_spec, q_shape = row, jax.ShapeDtypeStruct((t, d), BF16)
        v_spec, v_shape = row, jax.ShapeDtypeStruct((t, d), BF16)
    return pl.pallas_call(
        functools.partial(_attn_prep_kernel, transposed=transposed),
        grid=(t // tm,),
        in_specs=[
            pl.BlockSpec((tm, d), lambda i: (i, col_q)),
            pl.BlockSpec((tm, d), lambda i: (i, col_k)),
            pl.BlockSpec((tm, d), lambda i: (i, col_v)),
            pl.BlockSpec((1, HEAD_W), lambda i: (0, 0)),
            pl.BlockSpec((1, HEAD_W), lambda i: (0, 0)),
        ],
        out_specs=[q_spec, row, row, v_spec, row],
        out_shape=[q_shape, jax.ShapeDtypeStruct((t, d), F32), jax.ShapeDtypeStruct((t, d), BF16), v_shape,
                   jax.ShapeDtypeStruct((t, d), F32)],
        compiler_params=_cparams(("parallel",)),
        name="attn_prep",
    )(proj, proj, proj, gq2, gk2)


def _delta_rule_kernel(q_ref, k_ref, v_ref, z_ref, beta_ref, gc_ref, s0_ref, ng_ref,
                       o_ref, sfin_ref, s_ref, *, chunk, n_chunks):
    n = pl.program_id(1)
    rows = q_ref.shape[0]
    n_heads = q_ref.shape[1] // HEAD_W

    @pl.when(n == 0)
    def _():
        s_ref[...] = s0_ref[...]

    lane = lax.broadcasted_iota(jnp.int32, (rows, LANES), 1)
    ri = lax.broadcasted_iota(jnp.int32, (chunk, chunk), 0)
    ci = lax.broadcasted_iota(jnp.int32, (chunk, chunk), 1)
    eye = ri == ci
    incl = ci <= ri
    strict = ci < ri
    eye_f = jnp.where(eye, 1.0, 0.0).astype(F32)
    beta_all = beta_ref[...]
    gc_all = gc_ref[...]
    n_stage = int(math.log2(chunk))
    heads = range(n_heads)
    chunks = range(rows // chunk)
    items = [(h, c) for c in chunks for h in heads]

    beta_h = [jnp.sum(jnp.where(lane == h, beta_all, 0.0), axis=-1, keepdims=True) for h in heads]
    g_h = [jnp.sum(jnp.where(lane == n_heads + h, gc_all, 0.0), axis=-1, keepdims=True) for h in heads]
    q16, k16, kq, decay, eg, glast, gcol, rhs = {}, {}, {}, {}, {}, {}, {}, {}
    for it in items:
        h, c = it
        sl = slice(h * HEAD_W, (h + 1) * HEAD_W)
        rs = slice(c * chunk, (c + 1) * chunk)
        q16[it] = q_ref[rs, sl]
        k16[it] = k_ref[rs, sl]
        k = k16[it].astype(F32)
        beta = beta_h[h][rs]
        gcol[it] = g_h[h][rs]
        grow = jnp.sum(jnp.where(eye, gcol[it], 0.0), axis=0, keepdims=True)
        decay[it] = jnp.where(incl, jnp.exp(jnp.where(incl, gcol[it] - grow, 0.0)), 0.0)
        eg[it] = jnp.exp(gcol[it])
        glast[it] = gcol[it][chunk - 1:chunk]
        kb = k * beta
        kq[it] = jnp.concatenate([kb.astype(BF16), q16[it]], axis=0)
        rhs[it] = jnp.concatenate([(kb * eg[it]).astype(BF16),
                                   (v_ref[rs, sl].astype(F32) * beta).astype(BF16)], axis=1)
    kk = {it: _dot_nt(kq[it], k16[it]) for it in items}
    mp = {it: -jnp.where(strict, kk[it][0:chunk] * decay[it], 0.0) for it in items}
    tmat = {it: eye_f + mp[it] for it in items}
    for stage in range(1, n_stage):
        m16 = {it: mp[it].astype(BF16) for it in items}
        mp = {it: _dot(m16[it], m16[it]) for it in items}
        m16 = {it: mp[it].astype(BF16) for it in items}
        tmat = {it: tmat[it] + _dot(m16[it], tmat[it].astype(BF16)) for it in items}
    wu = {it: _dot(tmat[it].astype(BF16), rhs[it]) for it in items}
    intra16 = {it: (kk[it][chunk:2 * chunk] * decay[it]).astype(BF16) for it in items}

    s = [s_ref[h] for h in heads]
    for c in chunks:
        s16 = [s[h].astype(BF16) for h in heads]
        lhs = [jnp.concatenate([wu[(h, c)][:, 0:HEAD_W].astype(BF16),
                                (q16[(h, c)].astype(F32) * eg[(h, c)]).astype(BF16)], axis=0) for h in heads]
        ws = [_dot(lhs[h], s16[h]) for h in heads]
        vn16 = [(wu[(h, c)][:, HEAD_W:2 * HEAD_W] - ws[h][0:chunk]).astype(BF16) for h in heads]
        o = [ws[h][chunk:2 * chunk] + _dot(intra16[(h, c)], vn16[h]) for h in heads]
        kd16 = [(k16[(h, c)].astype(F32) * jnp.exp(glast[(h, c)] - gcol[(h, c)])).astype(BF16) for h in heads]
        s = [s[h] * jnp.exp(glast[(h, c)]) + _dot_tn(kd16[h], vn16[h]) for h in heads]
        for h in heads:
            sl = slice(h * HEAD_W, (h + 1) * HEAD_W)
            rs = slice(c * chunk, (c + 1) * chunk)
            on = o[h] * lax.rsqrt(jnp.mean(o[h] * o[h], axis=-1, keepdims=True) + RMS_EPS) * ng_ref[...]
            o_ref[rs, sl] = (on * _silu(z_ref[rs, sl])).astype(BF16)
    for h in heads:
        s_ref[h] = s[h]

    @pl.when(n == n_chunks - 1)
    def _():
        sfin_ref[...] = s_ref[...]


def _delta_rule(dq, dk, dv, proj, beta, gc, s0, ng, seq_len, chunk, rows, col_z):
    t, d = dq.shape
    nb = t // seq_len
    n_steps = seq_len // rows
    n_heads = d // HEAD_W
    kern = functools.partial(_delta_rule_kernel, chunk=chunk, n_chunks=n_steps)
    tok = lambda b, n: (b * n_steps + n, 0)
    return pl.pallas_call(
        kern,
        grid=(nb, n_steps),
        in_specs=[
            pl.BlockSpec((rows, d), tok),
            pl.BlockSpec((rows, d), tok),
            pl.BlockSpec((rows, d), tok),
            pl.BlockSpec((rows, d), lambda b, n: (b * n_steps + n, col_z)),
            pl.BlockSpec((rows, LANES), tok),
            pl.BlockSpec((rows, LANES), tok),
            pl.BlockSpec((None, n_heads, HEAD_W, HEAD_W), lambda b, n: (b, 0, 0, 0)),
            pl.BlockSpec((1, HEAD_W), lambda b, n: (0, 0)),
        ],
        out_specs=[
            pl.BlockSpec((rows, d), tok),
            pl.BlockSpec((None, n_heads, HEAD_W, HEAD_W), lambda b, n: (b, 0, 0, 0)),
        ],
        out_shape=[jax.ShapeDtypeStruct((t, d), BF16),
                   jax.ShapeDtypeStruct((nb, n_heads, HEAD_W, HEAD_W), F32)],
        scratch_shapes=[pltpu.VMEM((n_heads, HEAD_W, HEAD_W), F32)],
        compiler_params=_cparams(("parallel", "arbitrary")),
        name="delta_rule",
    )(dq, dk, dv, proj, beta, gc, s0, ng)


def _lambda(lam_ref, lam_init):
    l = lam_ref[...]
    a = jnp.sum(l[0:1] * l[1:2], axis=-1, keepdims=True)
    b = jnp.sum(l[2:3] * l[3:4], axis=-1, keepdims=True)
    return jnp.exp(a) - jnp.exp(b) + lam_init


def _sub_norm(o, g, lam_init):
    return o * lax.rsqrt(jnp.mean(o * o, axis=-1, keepdims=True) + RMS_EPS) * g * (1.0 - lam_init)


def _attn_prompt_kernel(qt_ref, k_ref, vt_ref, lam_ref, ng_ref, o_ref, m_ref, acc_ref, sa_ref, sb_ref, *, lam_init):
    i = pl.program_id(2)
    bq = qt_ref.shape[1]
    bk = vt_ref.shape[-1]
    assert bq == 2 * bk
    qt = qt_ref[...]
    row = lax.broadcasted_iota(jnp.int32, qt.shape, 0)
    zero = jnp.zeros_like(qt)
    qm = (jnp.where(row < DIFF_QK, qt, zero), jnp.where(row >= DIFF_QK, qt, zero))
    m_ref[...] = jnp.full(m_ref.shape, NEG_INF, F32)
    acc_ref[...] = jnp.zeros(acc_ref.shape, F32)

    def scores(j, s_ref):
        kb = k_ref[pl.ds(pl.multiple_of(j * bk, bk), bk), :]
        for m in range(2):
            s_ref[m] = _dot(kb, qm[m])

    def absorb(j, s_ref, mask):
        vb = vt_ref[j]
        for m in range(2):
            s = s_ref[m]
            if mask is not None:
                s = jnp.where(mask, s, NEG_INF)
            m_prev = m_ref[m]
            m_new = jnp.maximum(m_prev, jnp.max(s, axis=0, keepdims=True))
            alpha = jnp.exp(m_prev - m_new)
            p = jnp.exp(s - m_new).astype(BF16)
            acc_ref[m] = alpha * acc_ref[m] + _dot(vb, p)
            m_ref[m] = m_new

    scores(0, sa_ref)

    def body(jj, carry):
        scores(2 * jj + 1, sb_ref)
        absorb(2 * jj, sa_ref, None)
        scores(2 * jj + 2, sa_ref)
        absorb(2 * jj + 1, sb_ref, None)
        return carry

    lax.fori_loop(0, i, body, 0)
    scores(2 * i + 1, sb_ref)
    ri = lax.broadcasted_iota(jnp.int32, (bk, bq), 0)
    ci = lax.broadcasted_iota(jnp.int32, (bk, bq), 1)
    absorb(2 * i, sa_ref, (ri // CHUNK) <= (ci // CHUNK))
    absorb(2 * i + 1, sb_ref, ((ri + bk) // CHUNK) <= (ci // CHUNK))

    lam = _lambda(lam_ref, lam_init)
    a0 = acc_ref[0]
    a1 = acc_ref[1]
    ot = a0[0:HEAD_W] / a0[HEAD_W:HEAD_W + 1] - lam * (a1[0:HEAD_W] / a1[HEAD_W:HEAD_W + 1])
    o_ref[...] = _sub_norm(ot.T, ng_ref[...], lam_init).astype(BF16)


def _attn_prompt(qt, ka, vt, lam4, ng, seq_len, bk, lam_init):
    d, t = qt.shape
    nb = t // seq_len
    bq = 2 * bk
    nq = seq_len // bq
    nk = seq_len // bk
    n_heads = d // HEAD_W
    kern = functools.partial(_attn_prompt_kernel, lam_init=lam_init)
    return pl.pallas_call(
        kern,
        grid=(nb, n_heads, nq),
        in_specs=[
            pl.BlockSpec((HEAD_W, bq), lambda b, h, i: (h, b * nq + i)),
            pl.BlockSpec((seq_len, HEAD_W), lambda b, h, i: (b, h)),
            pl.BlockSpec((None, None, nk, V_AUG, bk), lambda b, h, i: (b, h, 0, 0, 0)),
            pl.BlockSpec((4, DIFF_QK), lambda b, h, i: (0, 0)),
            pl.BlockSpec((1, HEAD_W), lambda b, h, i: (0, 0)),
        ],
        out_specs=pl.BlockSpec((bq, HEAD_W), lambda b, h, i: (b * nq + i, h)),
        out_shape=jax.ShapeDtypeStruct((t, d), BF16),
        scratch_shapes=[pltpu.VMEM((2, 1, bq), F32), pltpu.VMEM((2, V_AUG, bq), F32),
                        pltpu.VMEM((2, bk, bq), F32), pltpu.VMEM((2, bk, bq), F32)],
        compiler_params=_cparams(("parallel", "parallel", "arbitrary")),
        name="attn_prompt",
    )(qt, ka, vt, lam4, ng)


def _attn_cached_kernel(q_ref, kn_ref, vn_ref, ck_ref, cv_ref, lam_ref, ng_ref, o_ref, *, lam_init):
    q = q_ref[...]
    lane = lax.broadcasted_iota(jnp.int32, q.shape, 1)
    zero = jnp.zeros_like(q)
    kp = ck_ref[...].astype(BF16)
    vp = cv_ref[...].astype(BF16)
    kn = kn_ref[...]
    vn = vn_ref[...]
    outs = []
    for m in range(2):
        qm = jnp.where((lane < DIFF_QK) if m == 0 else (lane >= DIFF_QK), q, zero)
        sp = _dot_nt(qm, kp)
        sn = _dot_nt(qm, kn)
        mx = jnp.maximum(jnp.max(sp, axis=-1, keepdims=True), jnp.max(sn, axis=-1, keepdims=True))
        pp = jnp.exp(sp - mx)
        pn = jnp.exp(sn - mx)
        den = jnp.sum(pp, axis=-1, keepdims=True) + jnp.sum(pn, axis=-1, keepdims=True)
        outs.append((_dot(pp.astype(BF16), vp) + _dot(pn.astype(BF16), vn)) / den)
    lam = _lambda(lam_ref, lam_init)
    o = outs[0] - lam * outs[1]
    o_ref[...] = _sub_norm(o, ng_ref[...], lam_init).astype(BF16)


def _attn_cached(qa, ka, va, cache_k, cache_v, lam4, ng, seq_len, lam_init):
    t, d = qa.shape
    nb = t // seq_len
    past = cache_k.shape[1]
    n_heads = d // HEAD_W
    kern = functools.partial(_attn_cached_kernel, lam_init=lam_init)
    tok = lambda b, h: (b, h)
    return pl.pallas_call(
        kern,
        grid=(nb, n_heads),
        in_specs=[
            pl.BlockSpec((seq_len, HEAD_W), tok),
            pl.BlockSpec((seq_len, HEAD_W), tok),
            pl.BlockSpec((seq_len, HEAD_W), tok),
            pl.BlockSpec((None, past, HEAD_W), lambda b, h: (b, 0, h)),
            pl.BlockSpec((None, past, HEAD_W), lambda b, h: (b, 0, h)),
            pl.BlockSpec((4, DIFF_QK), lambda b, h: (0, 0)),
            pl.BlockSpec((1, HEAD_W), lambda b, h: (0, 0)),
        ],
        out_specs=pl.BlockSpec((seq_len, HEAD_W), tok),
        out_shape=jax.ShapeDtypeStruct((t, d), BF16),
        compiler_params=_cparams(("parallel", "parallel")),
        name="attn_cached",
    )(qa, ka, va, cache_k, cache_v, lam4, ng)


def _merge_out_kernel(x_ref, oa_ref, ob_ref, ga_ref, gb_ref, wo_ref, g2_ref, h_ref, xnt_ref):
    merged = (_sigmoid(ga_ref[...]) * oa_ref[...].astype(F32)
              + _sigmoid(gb_ref[...]) * ob_ref[...].astype(F32))
    h = x_ref[...] + _dot(merged.astype(BF16), wo_ref[...])
    h_ref[...] = h
    hn = h * lax.rsqrt(jnp.mean(h * h, axis=-1, keepdims=True) + RMS_EPS) * g2_ref[...]
    xnt_ref[...] = hn.T.astype(BF16)


def _merge_out(x, oa, ob, proj, w_out, g2, tm, col_ga, col_gb):
    t, d = x.shape
    row = lambda i: (i, 0)
    return pl.pallas_call(
        _merge_out_kernel,
        grid=(t // tm,),
        in_specs=[
            pl.BlockSpec((tm, d), row),
            pl.BlockSpec((tm, d), row),
            pl.BlockSpec((tm, d), row),
            pl.BlockSpec((tm, d), lambda i: (i, col_ga)),
            pl.BlockSpec((tm, d), lambda i: (i, col_gb)),
            pl.BlockSpec((d, d), lambda i: (0, 0)),
            pl.BlockSpec((1, d), lambda i: (0, 0)),
        ],
        out_specs=[pl.BlockSpec((tm, d), row), pl.BlockSpec((d, tm), lambda i: (0, i))],
        out_shape=[jax.ShapeDtypeStruct((t, d), F32), jax.ShapeDtypeStruct((d, t), BF16)],
        compiler_params=_cparams(("parallel",)),
        name="merge_out",
    )(x, oa, ob, proj, proj, w_out, g2)


def _top_values(parts, count):
    vals = []
    for r in range(count):
        m = parts[0]
        for p in parts[1:]:
            m = jnp.maximum(m, p)
        m = jnp.max(m, axis=0, keepdims=True)
        vals.append(m)
        if r + 1 < count:
            parts = [jnp.where(p == m, NEG_INF, p) for p in parts]
    return vals


def _pair_candidates(sv0, sv1, count):
    k = sv0.shape[0]
    parts = []
    first_single = k
    for i in range(k):
        nj = min(k, count // (i + 1))
        if nj > 1:
            for j0 in range(0, nj, SUBLANES):
                parts.append(sv0[i:i + 1] + sv1[j0:j0 + SUBLANES])
        else:
            first_single = min(first_single, i)
    for i0 in range(first_single, k, SUBLANES):
        parts.append(sv0[i0:i0 + SUBLANES] + sv1[0:1])
    return parts


def _peer_route_kernel(xnt_ref, wqt_ref, sk_ref, s0_ref, s1_ref, tau_ref):
    pq = _dot(wqt_ref[...], xnt_ref[...]).astype(BF16)
    dh = sk_ref.shape[-1]
    taus = []
    for h in range(PEER_HEADS):
        sc = []
        for p in range(2):
            r0 = (h * 2 + p) * dh
            sc.append(_dot(sk_ref[h, p], pq[r0:r0 + dh]))
        sv0 = jnp.concatenate(_top_values([sc[0]], PEER_TOPK), axis=0)
        sv1 = jnp.concatenate(_top_values([sc[1]], PEER_TOPK), axis=0)
        tv = _top_values(_pair_candidates(sv0, sv1, PEER_TOPK + 1), PEER_TOPK + 1)
        mx = tv[0]
        z = jnp.exp(tv[0] - mx)
        for r in range(1, PEER_TOPK):
            z = z + jnp.exp(tv[r] - mx)
        logz = jnp.log(z)
        taus.append((0.5 * (tv[PEER_TOPK - 1] + tv[PEER_TOPK]) - mx - logz) * LOG2_E)
        s0_ref[h] = (sc[0] - sv0[0:1] - logz) * LOG2_E
        s1_ref[h] = (sc[1] - sv1[0:1]) * LOG2_E
    tau_ref[...] = jnp.concatenate(taus, axis=0)


def _peer_route(xnt, wqt, sk16, tm):
    d, t = xnt.shape
    qw = wqt.shape[0]
    tok = lambda i: (0, 0, i)
    return pl.pallas_call(
        _peer_route_kernel,
        grid=(t // tm,),
        in_specs=[
            pl.BlockSpec((d, tm), lambda i: (0, i)),
            pl.BlockSpec((qw, d), lambda i: (0, 0)),
            pl.BlockSpec(sk16.shape, lambda i: (0, 0, 0, 0)),
        ],
        out_specs=[
            pl.BlockSpec((PEER_HEADS, N_KEYS, tm), tok),
            pl.BlockSpec((PEER_HEADS, N_KEYS, tm), tok),
            pl.BlockSpec((PEER_HEADS, tm), lambda i: (0, i)),
        ],
        out_shape=[jax.ShapeDtypeStruct((PEER_HEADS, N_KEYS, t), F32),
                   jax.ShapeDtypeStruct((PEER_HEADS, N_KEYS, t), F32),
                   jax.ShapeDtypeStruct((PEER_HEADS, t), F32)],
        compiler_params=_cparams(("parallel",)),
        name="peer_route",
    )(xnt, wqt, sk16)


def _gelu(x):
    return 0.5 * x * (1.0 + lax.erf(x * (2.0 ** -0.5)))


GROUP_ROWS = 64


BF16_ROWS = 16


def _peer_mix_kernel(xnt_ref, u_ref, vt_ref, s0_ref, s1_ref, tau_ref, h_ref, y_ref, acc_ref, e1_ref, *, n_eblk):
    j = pl.program_id(1)
    na = s0_ref.shape[1]
    tm = xnt_ref.shape[1]

    @pl.when(j == 0)
    def _():
        acc_ref[...] = jnp.zeros(acc_ref.shape, F32)
        e1_ref[...] = jnp.exp2(s1_ref[...]).astype(BF16)

    xnt = xnt_ref[...]
    sub = 2 * N_KEYS
    n_sub = na * N_KEYS // sub
    lw = min(tm, LANES)
    grp = GROUP_ROWS // BF16_ROWS

    def activations(k):
        return _dot(u_ref[k * sub:(k + 1) * sub, :], xnt)

    def row16(x):
        return jnp.broadcast_to(x, (BF16_ROWS, lw)).astype(BF16)[None]

    act = activations(0)
    out = None
    for k in range(n_sub):
        act_next = activations(k + 1) if k + 1 < n_sub else None
        slabs = []
        for a in range(k * sub // N_KEYS, (k + 1) * sub // N_KEYS):
            for b0 in range(0, N_KEYS, GROUP_ROWS):
                bs = slice(b0, b0 + GROUP_ROWS)
                r0 = a * N_KEYS + b0 - k * sub
                cols = []
                for t0 in range(0, tm, lw):
                    ts = slice(t0, t0 + lw)
                    wgt = jnp.zeros((grp, BF16_ROWS, lw), BF16)
                    for h in range(PEER_HEADS):
                        s0 = s0_ref[h, a:a + 1, ts]
                        e1 = e1_ref[h, bs, ts].reshape(grp, BF16_ROWS, lw)
                        e1_min = row16(jnp.exp2(tau_ref[h:h + 1, ts] - s0))
                        wgt = wgt + jnp.where(e1 >= e1_min, e1, jnp.zeros_like(e1)) * row16(jnp.exp2(s0))
                    gate = _gelu(act[r0:r0 + GROUP_ROWS, ts]).astype(BF16)
                    cols.append(wgt.reshape(GROUP_ROWS, lw) * gate)
                slabs.append(jnp.concatenate(cols, axis=1))
        part = _dot(vt_ref[:, k * sub:(k + 1) * sub], jnp.concatenate(slabs, axis=0))
        out = part if out is None else out + part
        act = act_next
    acc_ref[...] += out

    @pl.when(j == n_eblk - 1)
    def _():
        y_ref[...] = h_ref[...] + acc_ref[...].T


def _peer_mix(xnt, u16, vt16, s0z, s1, tau, h, tm, te):
    d, t = xnt.shape
    n_exp = u16.shape[0]
    na = te // N_KEYS
    n_eblk = n_exp // te
    kern = functools.partial(_peer_mix_kernel, n_eblk=n_eblk)
    return pl.pallas_call(
        kern,
        grid=(t // tm, n_eblk),
        in_specs=[
            pl.BlockSpec((d, tm), lambda i, j: (0, i)),
            pl.BlockSpec((te, d), lambda i, j: (j, 0)),
            pl.BlockSpec((d, te), lambda i, j: (0, j)),
            pl.BlockSpec((PEER_HEADS, na, tm), lambda i, j: (0, j, i)),
            pl.BlockSpec((PEER_HEADS, N_KEYS, tm), lambda i, j: (0, 0, i)),
            pl.BlockSpec((PEER_HEADS, tm), lambda i, j: (0, i)),
            pl.BlockSpec((tm, d), lambda i, j: (i, 0)),
        ],
        out_specs=pl.BlockSpec((tm, d), lambda i, j: (i, 0)),
        out_shape=jax.ShapeDtypeStruct((t, d), F32),
        scratch_shapes=[pltpu.VMEM((d, tm), F32), pltpu.VMEM((PEER_HEADS, N_KEYS, tm), BF16)],
        compiler_params=_cparams(("parallel", "arbitrary")),
        name="peer_mix",
    )(xnt, u16, vt16, s0z, s1, tau, h)


def _pick(n, pref):
    t = min(n, pref)
    while n % t:
        t //= 2
    return t


def _layer(x, conv_buf, s0, past_k, past_v, lam_init, wts):
    nb, seq_len, d = x.shape
    t = nb * seq_len
    n_heads = d // HEAD_W
    qkv_w = 3 * d
    x2 = x.reshape(t, d)

    tm = _pick(t, 512)
    proj, proj_ba = _in_proj(x2, wts["norm_mix_g"], wts["w_main"], wts["w_ba"], tm, 1024)
    col ={name: idx for idx, name in enumerate(("q", "k", "v", "z", "fq", "fk", "fv", "ga", "gb"))}

    chunk = min(CHUNK, seq_len)
    tp = _pick(seq_len, 256)
    cbuf8 = jnp.concatenate(
        [jnp.zeros((nb, SUBLANES - (CONV_W - 1), qkv_w), F32), conv_buf.astype(F32)], axis=1)
    dq, dk, dv, beta, gc = _delta_prep(proj, proj_ba, cbuf8, wts["conv_w"], wts["alog_pad"], wts["dtb_pad"],
                                       seq_len, tp, chunk)
    prompt = past_k is None
    bk = _pick(seq_len // 2, 512)
    qa, kf, ka, va, vf = _attn_prep(proj, wts["gq2"], wts["gk2"], d, bk if prompt else _pick(t, 256),
                                col["fq"], col["fk"], col["fv"], seq_len, prompt)

    rows = min(seq_len, 2 * chunk)
    oa, s_new = _delta_rule(dq, dk, dv, proj, beta, gc, s0, wts["delta_norm_g"], seq_len, chunk, rows, col["z"])

    if prompt:
        ob = _attn_prompt(qa, ka, va, wts["lam4"], wts["diff_norm_g"], seq_len, bk, lam_init)
    else:
        ob = _attn_cached(qa, ka, va, past_k.reshape(nb, past_k.shape[1], d),
                          past_v.reshape(nb, past_v.shape[1], d), wts["lam4"], wts["diff_norm_g"],
                          seq_len, lam_init)

    tmo = _pick(t, 256)
    h, xnt = _merge_out(x2, oa, ob, proj, wts["w_out"], wts["norm_ffn_g"], tmo, col["ga"], col["gb"])
    s0z, s1, tau = _peer_route(xnt, wts["wqt"], wts["sk"], _pick(t, 256))
    y = _peer_mix(xnt, wts["u"], wts["vt"], s0z, s1, tau, h, _pick(t, 512), 1024)

    new_k = kf.reshape(nb, seq_len, n_heads, 2, DIFF_QK)
    new_v = vf.reshape(nb, seq_len, n_heads, HEAD_W)
    assert seq_len >= CONV_W - 1
    new_conv = proj[:, :qkv_w].reshape(nb, seq_len, qkv_w)[:, seq_len - (CONV_W - 1):]
    return y.reshape(nb, seq_len, d), new_k, new_v, s_new, new_conv


def _prep_weights(l, d, norm_mix_g, w_in, conv_w, a_log, dt_bias, delta_norm_g, q_norm_g, k_norm_g,
                  lq1, lk1, lq2, lk2, diff_norm_g, w_out, norm_ffn_g, peer_w_q, peer_sub_keys, peer_u, peer_v):
    n_heads = d // HEAD_W
    qkv_w = 3 * d
    w = w_in[l]
    o_z = qkv_w
    o_beta = o_z + d
    o_a = o_beta + n_heads
    o_fq = o_a + n_heads
    w_main = jnp.concatenate([w[:, :o_beta], w[:, o_fq:]], axis=1).astype(BF16)
    w_ba = jnp.pad(w[:, o_beta:o_fq], ((0, 0), (0, LANES - 2 * n_heads))).astype(BF16)

    def lane_pad(v):
        return jnp.pad(v[l].astype(F32), (n_heads, LANES - 2 * n_heads)).reshape(1, LANES)

    return {
        "norm_mix_g": norm_mix_g[l].reshape(1, d),
        "w_main": w_main,
        "w_ba": w_ba,
        "conv_w": conv_w[l],
        "alog_pad": lane_pad(a_log),
        "dtb_pad": lane_pad(dt_bias),
        "delta_norm_g": delta_norm_g[l].reshape(1, HEAD_W),
        "gq2": jnp.tile(q_norm_g[l], 2).reshape(1, HEAD_W),
        "gk2": jnp.tile(k_norm_g[l], 2).reshape(1, HEAD_W),
        "lam4": jnp.stack([lq1[l], lk1[l], lq2[l], lk2[l]]).astype(F32),
        "diff_norm_g": diff_norm_g[l].reshape(1, HEAD_W),
        "w_out": w_out[l].astype(BF16),
        "norm_ffn_g": norm_ffn_g[l].reshape(1, d),
        "wqt": peer_w_q[l].T.astype(BF16),
        "sk": peer_sub_keys[l].astype(BF16),
        "u": peer_u[l].astype(BF16),
        "vt": peer_v[l].T.astype(BF16),
    }


def kernel(x_prompt, x_sample, cache_diff_k, cache_diff_v, state_delta_s, state_delta_conv, norm_mix_g, w_in, conv_w, delta_a_log, delta_dt_bias, delta_norm_g, diff_q_norm_g, diff_k_norm_g, diff_lambda_q1, diff_lambda_k1, diff_lambda_q2, diff_lambda_k2, diff_norm_g, w_out, norm_ffn_g, peer_w_q, peer_sub_keys, peer_u, peer_v):
    depth = w_in.shape[0]
    d = x_prompt.shape[-1]
    n_heads = d // HEAD_W
    yp, ys = x_prompt, x_sample
    outs_p, outs_s = [], []
    for l in range(depth):
        wts = _prep_weights(l, d, norm_mix_g, w_in, conv_w, delta_a_log, delta_dt_bias, delta_norm_g,
                            diff_q_norm_g, diff_k_norm_g, diff_lambda_q1, diff_lambda_k1, diff_lambda_q2,
                            diff_lambda_k2, diff_norm_g, w_out, norm_ffn_g, peer_w_q, peer_sub_keys,
                            peer_u, peer_v)
        lam_init = 0.8 - 0.6 * math.exp(-0.3 * l)
        conv0 = jnp.zeros((yp.shape[0], CONV_W - 1, 3 * d), F32)
        s0 = jnp.zeros((yp.shape[0], n_heads, HEAD_W, HEAD_W), F32)
        yp, *rest = _layer(yp, conv0, s0, None, None, lam_init, wts)
        outs_p.append(rest)
        ys, *rest = _layer(ys, state_delta_conv[l], state_delta_s[l], cache_diff_k[l], cache_diff_v[l],
                           lam_init, wts)
        outs_s.append(rest)
    stack = lambda outs, i: jnp.stack([o[i] for o in outs])
    return (yp, ys,
            stack(outs_p, 0), stack(outs_p, 1), stack(outs_p, 2), stack(outs_p, 3),
            stack(outs_s, 0), stack(outs_s, 1), stack(outs_s, 2), stack(outs_s, 3))
```

```python
import functools
import math

import jax
import jax.numpy as jnp
from jax import lax
from jax.experimental import pallas as pl
from jax.experimental.pallas import tpu as pltpu

F32 = jnp.float32
BF16 = jnp.bfloat16

RMS_EPS = 1e-6
NEG_INF = -1e30
LOG2_E = math.log2(math.e)
CHUNK = 64
HEAD_W = 128
N_HEADS = 8
DIFF_QK = 64
CONV_W = 4
PEER_HEADS = 8
N_KEYS = 128
PEER_TOPK = 16
LANES = 128
SUBLANES = 8
VMEM_LIMIT = 56 * 1024 * 1024


def _cparams(sem):
    return pltpu.CompilerParams(dimension_semantics=sem, vmem_limit_bytes=VMEM_LIMIT)


def _dot(a, b):
    return jnp.dot(a, b, preferred_element_type=F32)


def _dot_nt(a, b):
    return lax.dot_general(a, b, (((1,), (1,)), ((), ())), preferred_element_type=F32)


def _dot_tn(a, b):
    return lax.dot_general(a, b, (((0,), (0,)), ((), ())), preferred_element_type=F32)


def _sigmoid(x):
    return 1.0 / (1.0 + jnp.exp(-x))


def _silu(x):
    return x * _sigmoid(x)


def _in_proj_kernel(x_ref, g_ref, w_ref, wba_ref, o_ref, oba_ref, xn_ref):
    @pl.when(pl.program_id(1) == 0)
    def _():
        x = x_ref[...]
        r = lax.rsqrt(jnp.mean(x * x, axis=-1, keepdims=True) + RMS_EPS)
        xn = (x * r * g_ref[...]).astype(BF16)
        xn_ref[...] = xn
        oba_ref[...] = _dot(xn, wba_ref[...])

    o_ref[...] = _dot(xn_ref[...], w_ref[...])


def _in_proj(x, g, w_main, w_ba, tm, tn):
    t, d = x.shape
    nw = w_main.shape[1]
    return pl.pallas_call(
        _in_proj_kernel,
        grid=(t // tm, nw // tn),
        in_specs=[
            pl.BlockSpec((tm, d), lambda i, j: (i, 0)),
            pl.BlockSpec((1, d), lambda i, j: (0, 0)),
            pl.BlockSpec((d, tn), lambda i, j: (0, j)),
            pl.BlockSpec((d, LANES), lambda i, j: (0, 0)),
        ],
        out_specs=[
            pl.BlockSpec((tm, tn), lambda i, j: (i, j)),
            pl.BlockSpec((tm, LANES), lambda i, j: (i, 0)),
        ],
        out_shape=[jax.ShapeDtypeStruct((t, nw), F32), jax.ShapeDtypeStruct((t, LANES), F32)],
        scratch_shapes=[pltpu.VMEM((tm, d), BF16)],
        compiler_params=_cparams(("parallel", "arbitrary")),
        name="in_proj",
    )(x, g, w_main, w_ba)


def _delta_prep_kernel(cur_ref, prev_ref, cbuf_ref, cw_ref, ba_ref, alog_ref, dtb_ref,
                       q_ref, k_ref, v_ref, beta_ref, gc_ref, *, tiles_per_seq, chunk):
    tm = cur_ref.shape[0]
    qkv_w = cur_ref.shape[1]
    d = qkv_w // 3
    first = (pl.program_id(0) % tiles_per_seq) == 0
    col_w = 4 * HEAD_W
    for c0 in range(0, qkv_w, col_w):
        cur = cur_ref[:, c0:c0 + col_w]
        prev = jnp.where(first, cbuf_ref[:, c0:c0 + col_w], prev_ref[:, c0:c0 + col_w])
        ext = jnp.concatenate([prev, cur], axis=0)
        conv = cw_ref[0:1, c0:c0 + col_w] * ext[SUBLANES - 3:SUBLANES - 3 + tm]
        for j in range(1, CONV_W):
            conv = conv + cw_ref[j:j + 1, c0:c0 + col_w] * ext[SUBLANES - 3 + j:SUBLANES - 3 + j + tm]
        act = _silu(conv)
        for hh in range(col_w // HEAD_W):
            col = c0 + hh * HEAD_W
            a = act[:, hh * HEAD_W:(hh + 1) * HEAD_W]
            if col < 2 * d:
                a = a * lax.rsqrt(jnp.sum(a * a, axis=-1, keepdims=True) + RMS_EPS)
            if col < d:
                q_ref[:, col:col + HEAD_W] = (a * (HEAD_W ** -0.5)).astype(BF16)
            elif col < 2 * d:
                k_ref[:, col - d:col - d + HEAD_W] = a.astype(BF16)
            else:
                v_ref[:, col - 2 * d:col - 2 * d + HEAD_W] = a.astype(BF16)
    ba = ba_ref[...]
    beta_ref[...] = _sigmoid(ba)
    xs = ba + dtb_ref[...]
    softplus = jnp.maximum(xs, 0.0) + jnp.log(1.0 + jnp.exp(-jnp.abs(xs)))
    g = -jnp.exp(alog_ref[...]) * softplus
    row = lax.broadcasted_iota(jnp.int32, (tm, tm), 0)
    colm = lax.broadcasted_iota(jnp.int32, (tm, tm), 1)
    tri = jnp.where((row // chunk == colm // chunk) & (colm <= row), 1.0, 0.0).astype(F32)
    gc_ref[...] = jnp.dot(tri, g, preferred_element_type=F32, precision=lax.Precision.HIGHEST)


def _delta_prep(proj, proj_ba, cbuf8, conv_w, alog_pad, dtb_pad, seq_len, tm, chunk):
    t = proj.shape[0]
    qkv_w = conv_w.shape[1]
    d = qkv_w // 3
    tiles_per_seq = seq_len // tm
    rows8 = tm // SUBLANES
    kern = functools.partial(_delta_prep_kernel, tiles_per_seq=tiles_per_seq, chunk=chunk)
    return pl.pallas_call(
        kern,
        grid=(t // tm,),
        in_specs=[
            pl.BlockSpec((tm, qkv_w), lambda i: (i, 0)),
            pl.BlockSpec((SUBLANES, qkv_w), lambda i: (jnp.maximum(i * rows8 - 1, 0), 0)),
            pl.BlockSpec((None, SUBLANES, qkv_w), lambda i: (i // tiles_per_seq, 0, 0)),
            pl.BlockSpec((CONV_W, qkv_w), lambda i: (0, 0)),
            pl.BlockSpec((tm, LANES), lambda i: (i, 0)),
            pl.BlockSpec((1, LANES), lambda i: (0, 0)),
            pl.BlockSpec((1, LANES), lambda i: (0, 0)),
        ],
        out_specs=[
            pl.BlockSpec((tm, d), lambda i: (i, 0)),
            pl.BlockSpec((tm, d), lambda i: (i, 0)),
            pl.BlockSpec((tm, d), lambda i: (i, 0)),
            pl.BlockSpec((tm, LANES), lambda i: (i, 0)),
            pl.BlockSpec((tm, LANES), lambda i: (i, 0)),
        ],
        out_shape=[jax.ShapeDtypeStruct((t, d), BF16)] * 3 + [jax.ShapeDtypeStruct((t, LANES), F32)] * 2,
        compiler_params=_cparams(("parallel",)),
        name="delta_prep",
    )(proj, proj, cbuf8, conv_w, proj_ba, alog_pad, dtb_pad)


V_AUG = HEAD_W + 16


def _attn_prep_kernel(fq_ref, fk_ref, fv_ref, gq_ref, gk_ref, q_ref, kf_ref, ka_ref, v_ref, vf_ref, *, transposed):
    tm, d = fq_ref.shape
    vf_ref[...] = fv_ref[...]
    lane = lax.broadcasted_iota(jnp.int32, (tm, HEAD_W), 1)
    low = lane < DIFF_QK

    def qk_norm(x, g):
        s = x * x
        tot = jnp.sum(s, axis=-1, keepdims=True)
        lo = jnp.sum(jnp.where(low, s, 0.0), axis=-1, keepdims=True)
        ms = jnp.where(low, lo, tot - lo) * (1.0 / DIFF_QK)
        return x * lax.rsqrt(ms + RMS_EPS) * g

    for h in range(d // HEAD_W):
        sl = slice(h * HEAD_W, (h + 1) * HEAD_W)
        qn = qk_norm(fq_ref[:, sl], gq_ref[...]) * (DIFF_QK ** -0.5)
        kn = qk_norm(fk_ref[:, sl], gk_ref[...])
        kf_ref[:, sl] = kn
        ka_ref[:, sl] = kn.astype(BF16)
        if transposed:
            q_ref[sl, :] = (qn * LOG2_E).T.astype(BF16)
            v_ref[h, 0:HEAD_W, :] = fv_ref[:, sl].T.astype(BF16)
            v_ref[h, HEAD_W:V_AUG, :] = jnp.ones((V_AUG - HEAD_W, tm), BF16)
        else:
            q_ref[:, sl] = qn.astype(BF16)
            v_ref[:, sl] = fv_ref[:, sl].astype(BF16)


def _attn_prep(proj, gq2, gk2, d, tm, col_q, col_k, col_v, seq_len, transposed):
    t = proj.shape[0]
    n_heads = d // HEAD_W
    row = pl.BlockSpec((tm, d), lambda i: (i, 0))
    if transposed:
        per_seq = seq_len // tm
        q_spec = pl.BlockSpec((d, tm), lambda i: (0, i))
        q_shape = jax.ShapeDtypeStruct((d, t), BF16)
        v_spec = pl.BlockSpec((None, n_heads, None, V_AUG, tm), lambda i: (i // per_seq, 0, i % per_seq, 0, 0))
        v_shape = jax.ShapeDtypeStruct((t // seq_len, n_heads, per_seq, V_AUG, tm), BF16)
    else:
        q_spec, q_shape = row, jax.ShapeDtypeStruct((t, d), BF16)
        v_spec, v_shape = row, jax.ShapeDtypeStruct((t, d), BF16)
    return pl.pallas_call(
        functools.partial(_attn_prep_kernel, transposed=transposed),
        grid=(t // tm,),
        in_specs=[
            pl.BlockSpec((tm, d), lambda i: (i, col_q)),
            pl.BlockSpec((tm, d), lambda i: (i, col_k)),
            pl.BlockSpec((tm, d), lambda i: (i, col_v)),
            pl.BlockSpec((1, HEAD_W), lambda i: (0, 0)),
            pl.BlockSpec((1, HEAD_W), lambda i: (0, 0)),
        ],
        out_specs=[q_spec, row, row, v_spec, row],
        out_shape=[q_shape, jax.ShapeDtypeStruct((t, d), F32), jax.ShapeDtypeStruct((t, d), BF16), v_shape,
                   jax.ShapeDtypeStruct((t, d), F32)],
        compiler_params=_cparams(("parallel",)),
        name="attn_prep",
    )(proj, proj, proj, gq2, gk2)


def _delta_rule_kernel(q_ref, k_ref, v_ref, z_ref, beta_ref, gc_ref, s0_ref, ng_ref,
                       o_ref, sfin_ref, s_ref, *, chunk, n_chunks):
    n = pl.program_id(1)
    rows = q_ref.shape[0]
    n_heads = q_ref.shape[1] // HEAD_W

    @pl.when(n == 0)
    def _():
        s_ref[...] = s0_ref[...]

    lane = lax.broadcasted_iota(jnp.int32, (rows, LANES), 1)
    ri = lax.broadcasted_iota(jnp.int32, (chunk, chunk), 0)
    ci = lax.broadcasted_iota(jnp.int32, (chunk, chunk), 1)
    eye = ri == ci
    incl = ci <= ri
    strict = ci < ri
    eye_f = jnp.where(eye, 1.0, 0.0).astype(F32)
    beta_all = beta_ref[...]
    gc_all = gc_ref[...]
    n_stage = int(math.log2(chunk))
    heads = range(n_heads)
    chunks = range(rows // chunk)
    items = [(h, c) for c in chunks for h in heads]

    beta_h = [jnp.sum(jnp.where(lane == h, beta_all, 0.0), axis=-1, keepdims=True) for h in heads]
    g_h = [jnp.sum(jnp.where(lane == n_heads + h, gc_all, 0.0), axis=-1, keepdims=True) for h in heads]
    q16, k16, kq, decay, eg, glast, gcol, rhs = {}, {}, {}, {}, {}, {}, {}, {}
    for it in items:
        h, c = it
        sl = slice(h * HEAD_W, (h + 1) * HEAD_W)
        rs = slice(c * chunk, (c + 1) * chunk)
        q16[it] = q_ref[rs, sl]
        k16[it] = k_ref[rs, sl]
        k = k16[it].astype(F32)
        beta = beta_h[h][rs]
        gcol[it] = g_h[h][rs]
        grow = jnp.sum(jnp.where(eye, gcol[it], 0.0), axis=0, keepdims=True)
        decay[it] = jnp.where(incl, jnp.exp(jnp.where(incl, gcol[it] - grow, 0.0)), 0.0)
        eg[it] = jnp.exp(gcol[it])
        glast[it] = gcol[it][chunk - 1:chunk]
        kb = k * beta
        kq[it] = jnp.concatenate([kb.astype(BF16), q16[it]], axis=0)
        rhs[it] = jnp.concatenate([(kb * eg[it]).astype(BF16),
                                   (v_ref[rs, sl].astype(F32) * beta).astype(BF16)], axis=1)
    kk = {it: _dot_nt(kq[it], k16[it]) for it in items}
    mp = {it: -jnp.where(strict, kk[it][0:chunk] * decay[it], 0.0) for it in items}
    tmat = {it: eye_f + mp[it] for it in items}
    for stage in range(1, n_stage):
        m16 = {it: mp[it].astype(BF16) for it in items}
        mp = {it: _dot(m16[it], m16[it]) for it in items}
        m16 = {it: mp[it].astype(BF16) for it in items}
        tmat = {it: tmat[it] + _dot(m16[it], tmat[it].astype(BF16)) for it in items}
    wu = {it: _dot(tmat[it].astype(BF16), rhs[it]) for it in items}
    intra16 = {it: (kk[it][chunk:2 * chunk] * decay[it]).astype(BF16) for it in items}

    s = [s_ref[h] for h in heads]
    for c in chunks:
        s16 = [s[h].astype(BF16) for h in heads]
        lhs = [jnp.concatenate([wu[(h, c)][:, 0:HEAD_W].astype(BF16),
                                (q16[(h, c)].astype(F32) * eg[(h, c)]).astype(BF16)], axis=0) for h in heads]
        ws = [_dot(lhs[h], s16[h]) for h in heads]
        vn16 = [(wu[(h, c)][:, HEAD_W:2 * HEAD_W] - ws[h][0:chunk]).astype(BF16) for h in heads]
        o = [ws[h][chunk:2 * chunk] + _dot(intra16[(h, c)], vn16[h]) for h in heads]
        kd16 = [(k16[(h, c)].astype(F32) * jnp.exp(glast[(h, c)] - gcol[(h, c)])).astype(BF16) for h in heads]
        s = [s[h] * jnp.exp(glast[(h, c)]) + _dot_tn(kd16[h], vn16[h]) for h in heads]
        for h in heads:
            sl = slice(h * HEAD_W, (h + 1) * HEAD_W)
            rs = slice(c * chunk, (c + 1) * chunk)
            on = o[h] * lax.rsqrt(jnp.mean(o[h] * o[h], axis=-1, keepdims=True) + RMS_EPS) * ng_ref[...]
            o_ref[rs, sl] = (on * _silu(z_ref[rs, sl])).astype(BF16)
    for h in heads:
        s_ref[h] = s[h]

    @pl.when(n == n_chunks - 1)
    def _():
        sfin_ref[...] = s_ref[...]


def _delta_rule(dq, dk, dv, proj, beta, gc, s0, ng, seq_len, chunk, rows, col_z):
    t, d = dq.shape
    nb = t // seq_len
    n_steps = seq_len // rows
    n_heads = d // HEAD_W
    kern = functools.partial(_delta_rule_kernel, chunk=chunk, n_chunks=n_steps)
    tok = lambda b, n: (b * n_steps + n, 0)
    return pl.pallas_call(
        kern,
        grid=(nb, n_steps),
        in_specs=[
            pl.BlockSpec((rows, d), tok),
            pl.BlockSpec((rows, d), tok),
            pl.BlockSpec((rows, d), tok),
            pl.BlockSpec((rows, d), lambda b, n: (b * n_steps + n, col_z)),
            pl.BlockSpec((rows, LANES), tok),
            pl.BlockSpec((rows, LANES), tok),
            pl.BlockSpec((None, n_heads, HEAD_W, HEAD_W), lambda b, n: (b, 0, 0, 0)),
            pl.BlockSpec((1, HEAD_W), lambda b, n: (0, 0)),
        ],
        out_specs=[
            pl.BlockSpec((rows, d), tok),
            pl.BlockSpec((None, n_heads, HEAD_W, HEAD_W), lambda b, n: (b, 0, 0, 0)),
        ],
        out_shape=[jax.ShapeDtypeStruct((t, d), BF16),
                   jax.ShapeDtypeStruct((nb, n_heads, HEAD_W, HEAD_W), F32)],
        scratch_shapes=[pltpu.VMEM((n_heads, HEAD_W, HEAD_W), F32)],
        compiler_params=_cparams(("parallel", "arbitrary")),
        name="delta_rule",
    )(dq, dk, dv, proj, beta, gc, s0, ng)


def _lambda(lam_ref, lam_init):
    l = lam_ref[...]
    a = jnp.sum(l[0:1] * l[1:2], axis=-1, keepdims=True)
    b = jnp.sum(l[2:3] * l[3:4], axis=-1, keepdims=True)
    return jnp.exp(a) - jnp.exp(b) + lam_init


def _sub_norm(o, g, lam_init):
    return o * lax.rsqrt(jnp.mean(o * o, axis=-1, keepdims=True) + RMS_EPS) * g * (1.0 - lam_init)


def _attn_prompt_kernel(qt_ref, k_ref, vt_ref, lam_ref, ng_ref, o_ref, m_ref, acc_ref, sa_ref, sb_ref, *, lam_init):
    i = pl.program_id(2)
    bq = qt_ref.shape[1]
    bk = vt_ref.shape[-1]
    assert bq == 2 * bk
    qt = qt_ref[...]
    row = lax.broadcasted_iota(jnp.int32, qt.shape, 0)
    zero = jnp.zeros_like(qt)
    qm = (jnp.where(row < DIFF_QK, qt, zero), jnp.where(row >= DIFF_QK, qt, zero))
    m_ref[...] = jnp.full(m_ref.shape, NEG_INF, F32)
    acc_ref[...] = jnp.zeros(acc_ref.shape, F32)

    def scores(j, s_ref):
        kb = k_ref[pl.ds(pl.multiple_of(j * bk, bk), bk), :]
        for m in range(2):
            s_ref[m] = _dot(kb, qm[m])

    def absorb(j, s_ref, mask):
        vb = vt_ref[j]
        for m in range(2):
            s = s_ref[m]
            if mask is not None:
                s = jnp.where(mask, s, NEG_INF)
            m_prev = m_ref[m]
            m_new = jnp.maximum(m_prev, jnp.max(s, axis=0, keepdims=True))
            alpha = jnp.exp2(m_prev - m_new)
            p = jnp.exp2(s - m_new).astype(BF16)
            acc_ref[m] = alpha * acc_ref[m] + _dot(vb, p)
            m_ref[m] = m_new

    scores(0, sa_ref)

    def body(jj, carry):
        scores(2 * jj + 1, sb_ref)
        absorb(2 * jj, sa_ref, None)
        scores(2 * jj + 2, sa_ref)
        absorb(2 * jj + 1, sb_ref, None)
        return carry

    lax.fori_loop(0, i, body, 0)
    scores(2 * i + 1, sb_ref)
    ri = lax.broadcasted_iota(jnp.int32, (bk, bq), 0)
    ci = lax.broadcasted_iota(jnp.int32, (bk, bq), 1)
    absorb(2 * i, sa_ref, (ri // CHUNK) <= (ci // CHUNK))
    absorb(2 * i + 1, sb_ref, ((ri + bk) // CHUNK) <= (ci // CHUNK))

    lam = _lambda(lam_ref, lam_init)
    a0 = acc_ref[0]
    a1 = acc_ref[1]
    ot = a0[0:HEAD_W] / a0[HEAD_W:HEAD_W + 1] - lam * (a1[0:HEAD_W] / a1[HEAD_W:HEAD_W + 1])
    o_ref[...] = _sub_norm(ot.T, ng_ref[...], lam_init).astype(BF16)


def _attn_prompt(qt, ka, vt, lam4, ng, seq_len, bk, lam_init):
    d, t = qt.shape
    nb = t // seq_len
    bq = 2 * bk
    nq = seq_len // bq
    nk = seq_len // bk
    n_heads = d // HEAD_W
    kern = functools.partial(_attn_prompt_kernel, lam_init=lam_init)
    return pl.pallas_call(
        kern,
        grid=(nb, n_heads, nq),
        in_specs=[
            pl.BlockSpec((HEAD_W, bq), lambda b, h, i: (h, b * nq + i)),
            pl.BlockSpec((seq_len, HEAD_W), lambda b, h, i: (b, h)),
            pl.BlockSpec((None, None, nk, V_AUG, bk), lambda b, h, i: (b, h, 0, 0, 0)),
            pl.BlockSpec((4, DIFF_QK), lambda b, h, i: (0, 0)),
            pl.BlockSpec((1, HEAD_W), lambda b, h, i: (0, 0)),
        ],
        out_specs=pl.BlockSpec((bq, HEAD_W), lambda b, h, i: (b * nq + i, h)),
        out_shape=jax.ShapeDtypeStruct((t, d), BF16),
        scratch_shapes=[pltpu.VMEM((2, 1, bq), F32), pltpu.VMEM((2, V_AUG, bq), F32),
                        pltpu.VMEM((2, bk, bq), F32), pltpu.VMEM((2, bk, bq), F32)],
        compiler_params=_cparams(("parallel", "parallel", "arbitrary")),
        name="attn_prompt",
    )(qt, ka, vt, lam4, ng)


def _attn_cached_kernel(q_ref, kn_ref, vn_ref, ck_ref, cv_ref, lam_ref, ng_ref, o_ref, *, lam_init):
    q = q_ref[...]
    lane = lax.broadcasted_iota(jnp.int32, q.shape, 1)
    zero = jnp.zeros_like(q)
    kp = ck_ref[...].astype(BF16)
    vp = cv_ref[...].astype(BF16)
    kn = kn_ref[...]
    vn = vn_ref[...]
    outs = []
    for m in range(2):
        qm = jnp.where((lane < DIFF_QK) if m == 0 else (lane >= DIFF_QK), q, zero)
        sp = _dot_nt(qm, kp)
        sn = _dot_nt(qm, kn)
        mx = jnp.maximum(jnp.max(sp, axis=-1, keepdims=True), jnp.max(sn, axis=-1, keepdims=True))
        pp = jnp.exp(sp - mx)
        pn = jnp.exp(sn - mx)
        den = jnp.sum(pp, axis=-1, keepdims=True) + jnp.sum(pn, axis=-1, keepdims=True)
        outs.append((_dot(pp.astype(BF16), vp) + _dot(pn.astype(BF16), vn)) / den)
    lam = _lambda(lam_ref, lam_init)
    o = outs[0] - lam * outs[1]
    o_ref[...] = _sub_norm(o, ng_ref[...], lam_init).astype(BF16)


def _attn_cached(qa, ka, va, cache_k, cache_v, lam4, ng, seq_len, lam_init):
    t, d = qa.shape
    nb = t // seq_len
    past = cache_k.shape[1]
    n_heads = d // HEAD_W
    kern = functools.partial(_attn_cached_kernel, lam_init=lam_init)
    tok = lambda b, h: (b, h)
    return pl.pallas_call(
        kern,
        grid=(nb, n_heads),
        in_specs=[
            pl.BlockSpec((seq_len, HEAD_W), tok),
            pl.BlockSpec((seq_len, HEAD_W), tok),
            pl.BlockSpec((seq_len, HEAD_W), tok),
            pl.BlockSpec((None, past, HEAD_W), lambda b, h: (b, 0, h)),
            pl.BlockSpec((None, past, HEAD_W), lambda b, h: (b, 0, h)),
            pl.BlockSpec((4, DIFF_QK), lambda b, h: (0, 0)),
            pl.BlockSpec((1, HEAD_W), lambda b, h: (0, 0)),
        ],
        out_specs=pl.BlockSpec((seq_len, HEAD_W), tok),
        out_shape=jax.ShapeDtypeStruct((t, d), BF16),
        compiler_params=_cparams(("parallel", "parallel")),
        name="attn_cached",
    )(qa, ka, va, cache_k, cache_v, lam4, ng)


def _merge_out_kernel(x_ref, oa_ref, ob_ref, ga_ref, gb_ref, wo_ref, g2_ref, h_ref, xnt_ref):
    merged = (_sigmoid(ga_ref[...]) * oa_ref[...].astype(F32)
              + _sigmoid(gb_ref[...]) * ob_ref[...].astype(F32))
    h = x_ref[...] + _dot(merged.astype(BF16), wo_ref[...])
    h_ref[...] = h
    hn = h * lax.rsqrt(jnp.mean(h * h, axis=-1, keepdims=True) + RMS_EPS) * g2_ref[...]
    xnt_ref[...] = hn.T.astype(BF16)


def _merge_out(x, oa, ob, proj, w_out, g2, tm, col_ga, col_gb):
    t, d = x.shape
    row = lambda i: (i, 0)
    return pl.pallas_call(
        _merge_out_kernel,
        grid=(t // tm,),
        in_specs=[
            pl.BlockSpec((tm, d), row),
            pl.BlockSpec((tm, d), row),
            pl.BlockSpec((tm, d), row),
            pl.BlockSpec((tm, d), lambda i: (i, col_ga)),
            pl.BlockSpec((tm, d), lambda i: (i, col_gb)),
            pl.BlockSpec((d, d), lambda i: (0, 0)),
            pl.BlockSpec((1, d), lambda i: (0, 0)),
        ],
        out_specs=[pl.BlockSpec((tm, d), row), pl.BlockSpec((d, tm), lambda i: (0, i))],
        out_shape=[jax.ShapeDtypeStruct((t, d), F32), jax.ShapeDtypeStruct((d, t), BF16)],
        compiler_params=_cparams(("parallel",)),
        name="merge_out",
    )(x, oa, ob, proj, proj, w_out, g2)


def _top_values(parts, count):
    vals = []
    for r in range(count):
        m = parts[0]
        for p in parts[1:]:
            m = jnp.maximum(m, p)
        m = jnp.max(m, axis=0, keepdims=True)
        vals.append(m)
        if r + 1 < count:
            parts = [jnp.where(p == m, NEG_INF, p) for p in parts]
    return vals


def _sorting_network(n):
    pairs = []
    p = 1
    while p < n:
        k = p
        while k >= 1:
            for j in range(k % p, n - k, 2 * k):
                for i in range(min(k, n - j - k)):
                    if (i + j) // (2 * p) == (i + j + k) // (2 * p):
                        pairs.append((i + j, i + j + k))
            k //= 2
        p *= 2
    return pairs


def _top_values_sorted(x, count):
    m = x.shape[0] // SUBLANES
    assert count <= m
    tiles = [x[SUBLANES * v:SUBLANES * (v + 1)] for v in range(m)]
    for i, j in _sorting_network(m):
        tiles[i], tiles[j] = jnp.maximum(tiles[i], tiles[j]), jnp.minimum(tiles[i], tiles[j])
    sub = lax.broadcasted_iota(jnp.int32, tiles[0].shape, 0).astype(F32)
    vals = []
    for r in range(count):
        top = jnp.max(tiles[0], axis=0, keepdims=True)
        vals.append(top)
        live = count - r - 1
        if live > 0:
            first = jnp.min(jnp.where(tiles[0] == top, sub, float(SUBLANES)), axis=0, keepdims=True)
            pop = sub == first
            for v in range(live):
                tiles[v] = jnp.where(pop, tiles[v + 1], tiles[v])
    return vals


def _pair_candidates(sv0, sv1, count):
    k = sv0.shape[0]
    parts = []
    first_single = k
    for i in range(k):
        nj = min(k, count // (i + 1))
        if nj > 1:
            for j0 in range(0, nj, SUBLANES):
                parts.append(sv0[i:i + 1] + sv1[j0:j0 + SUBLANES])
        else:
            first_single = min(first_single, i)
    for i0 in range(first_single, k, SUBLANES):
        parts.append(sv0[i0:i0 + SUBLANES] + sv1[0:1])
    return parts


def _peer_route_kernel(xnt_ref, wqt_ref, sk_ref, s0_ref, s1_ref, tau_ref):
    pq = _dot(wqt_ref[...], xnt_ref[...]).astype(BF16)
    dh = sk_ref.shape[-1]
    taus = []
    for h in range(PEER_HEADS):
        sc = []
        for p in range(2):
            r0 = (h * 2 + p) * dh
            sc.append(_dot(sk_ref[h, p], pq[r0:r0 + dh]))
        sv0 = jnp.concatenate(_top_values_sorted(sc[0], PEER_TOPK), axis=0)
        sv1 = jnp.concatenate(_top_values_sorted(sc[1], PEER_TOPK), axis=0)
        tv = _top_values(_pair_candidates(sv0, sv1, PEER_TOPK + 1), PEER_TOPK + 1)
        mx = tv[0]
        z = jnp.exp(tv[0] - mx)
        for r in range(1, PEER_TOPK):
            z = z + jnp.exp(tv[r] - mx)
        logz = jnp.log(z)
        taus.append((0.5 * (tv[PEER_TOPK - 1] + tv[PEER_TOPK]) - mx - logz) * LOG2_E)
        s0_ref[h] = (sc[0] - sv0[0:1] - logz) * LOG2_E
        s1_ref[h] = (sc[1] - sv1[0:1]) * LOG2_E
    tau_ref[...] = jnp.concatenate(taus, axis=0)


def _peer_route(xnt, wqt, sk16, tm):
    d, t = xnt.shape
    qw = wqt.shape[0]
    tok = lambda i: (0, 0, i)
    return pl.pallas_call(
        _peer_route_kernel,
        grid=(t // tm,),
        in_specs=[
            pl.BlockSpec((d, tm), lambda i: (0, i)),
            pl.BlockSpec((qw, d), lambda i: (0, 0)),
            pl.BlockSpec(sk16.shape, lambda i: (0, 0, 0, 0)),
        ],
        out_specs=[
            pl.BlockSpec((PEER_HEADS, N_KEYS, tm), tok),
            pl.BlockSpec((PEER_HEADS, N_KEYS, tm), tok),
            pl.BlockSpec((PEER_HEADS, tm), lambda i: (0, i)),
        ],
        out_shape=[jax.ShapeDtypeStruct((PEER_HEADS, N_KEYS, t), F32),
                   jax.ShapeDtypeStruct((PEER_HEADS, N_KEYS, t), F32),
                   jax.ShapeDtypeStruct((PEER_HEADS, t), F32)],
        compiler_params=_cparams(("parallel",)),
        name="peer_route",
    )(xnt, wqt, sk16)


def _gelu(x):
    return 0.5 * x * (1.0 + lax.erf(x * (2.0 ** -0.5)))


GROUP_ROWS = 64


BF16_ROWS = 16


def _peer_mix_kernel(xnt_ref, u_ref, vt_ref, s0_ref, s1_ref, tau_ref, h_ref, y_ref, acc_ref, e1_ref, *, n_eblk):
    j = pl.program_id(1)
    na = s0_ref.shape[1]
    tm = xnt_ref.shape[1]

    @pl.when(j == 0)
    def _():
        acc_ref[...] = jnp.zeros(acc_ref.shape, F32)
        e1_ref[...] = jnp.exp2(s1_ref[...]).astype(BF16)

    xnt = xnt_ref[...]
    sub = 2 * N_KEYS
    n_sub = na * N_KEYS // sub
    lw = min(tm, LANES)
    grp = GROUP_ROWS // BF16_ROWS

    def activations(k):
        return _dot(u_ref[k * sub:(k + 1) * sub, :], xnt)

    def row16(x):
        return jnp.broadcast_to(x, (BF16_ROWS, lw)).astype(BF16)[None]

    act = activations(0)
    out = None
    for k in range(n_sub):
        act_next = activations(k + 1) if k + 1 < n_sub else None
        slabs = []
        for a in range(k * sub // N_KEYS, (k + 1) * sub // N_KEYS):
            for b0 in range(0, N_KEYS, GROUP_ROWS):
                bs = slice(b0, b0 + GROUP_ROWS)
                r0 = a * N_KEYS + b0 - k * sub
                cols = []
                for t0 in range(0, tm, lw):
                    ts = slice(t0, t0 + lw)
                    wgt = jnp.zeros((grp, BF16_ROWS, lw), BF16)
                    for h in range(PEER_HEADS):
                        s0 = s0_ref[h, a:a + 1, ts]
                        e1 = e1_ref[h, bs, ts].reshape(grp, BF16_ROWS, lw)
                        e1_min = row16(jnp.exp2(tau_ref[h:h + 1, ts] - s0))
                        wgt = wgt + jnp.where(e1 >= e1_min, e1, jnp.zeros_like(e1)) * row16(jnp.exp2(s0))
                    gate = _gelu(act[r0:r0 + GROUP_ROWS, ts]).astype(BF16)
                    cols.append(wgt.reshape(GROUP_ROWS, lw) * gate)
                slabs.append(jnp.concatenate(cols, axis=1))
        part = _dot(vt_ref[:, k * sub:(k + 1) * sub], jnp.concatenate(slabs, axis=0))
        out = part if out is None else out + part
        act = act_next
    acc_ref[...] += out

    @pl.when(j == n_eblk - 1)
    def _():
        y_ref[...] = h_ref[...] + acc_ref[...].T


def _peer_mix(xnt, u16, vt16, s0z, s1, tau, h, tm, te):
    d, t = xnt.shape
    n_exp = u16.shape[0]
    na = te // N_KEYS
    n_eblk = n_exp // te
    kern = functools.partial(_peer_mix_kernel, n_eblk=n_eblk)
    return pl.pallas_call(
        kern,
        grid=(t // tm, n_eblk),
        in_specs=[
            pl.BlockSpec((d, tm), lambda i, j: (0, i)),
            pl.BlockSpec((te, d), lambda i, j: (j, 0)),
            pl.BlockSpec((d, te), lambda i, j: (0, j)),
            pl.BlockSpec((PEER_HEADS, na, tm), lambda i, j: (0, j, i)),
            pl.BlockSpec((PEER_HEADS, N_KEYS, tm), lambda i, j: (0, 0, i)),
            pl.BlockSpec((PEER_HEADS, tm), lambda i, j: (0, i)),
            pl.BlockSpec((tm, d), lambda i, j: (i, 0)),
        ],
        out_specs=pl.BlockSpec((tm, d), lambda i, j: (i, 0)),
        out_shape=jax.ShapeDtypeStruct((t, d), F32),
        scratch_shapes=[pltpu.VMEM((d, tm), F32), pltpu.VMEM((PEER_HEADS, N_KEYS, tm), BF16)],
        compiler_params=_cparams(("parallel", "arbitrary")),
        name="peer_mix",
    )(xnt, u16, vt16, s0z, s1, tau, h)


def _pick(n, pref):
    t = min(n, pref)
    while n % t:
        t //= 2
    return t


def _layer(x, conv_buf, s0, past_k, past_v, lam_init, wts):
    nb, seq_len, d = x.shape
    t = nb * seq_len
    n_heads = d // HEAD_W
    qkv_w = 3 * d
    x2 = x.reshape(t, d)

    tm = _pick(t, 1024)
    proj, proj_ba = _in_proj(x2, wts["norm_mix_g"], wts["w_main"], wts["w_ba"], tm, 1024)
    col ={name: idx for idx, name in enumerate(("q", "k", "v", "z", "fq", "fk", "fv", "ga", "gb"))}

    chunk = min(CHUNK, seq_len)
    tp = _pick(seq_len, 256)
    cbuf8 = jnp.concatenate(
        [jnp.zeros((nb, SUBLANES - (CONV_W - 1), qkv_w), F32), conv_buf.astype(F32)], axis=1)
    dq, dk, dv, beta, gc = _delta_prep(proj, proj_ba, cbuf8, wts["conv_w"], wts["alog_pad"], wts["dtb_pad"],
                                       seq_len, tp, chunk)
    prompt = past_k is None
    bk = _pick(seq_len // 2, 512)
    qa, kf, ka, va, vf = _attn_prep(proj, wts["gq2"], wts["gk2"], d, bk if prompt else _pick(t, 256),
                                col["fq"], col["fk"], col["fv"], seq_len, prompt)

    rows = min(seq_len, 2 * chunk)
    oa, s_new = _delta_rule(dq, dk, dv, proj, beta, gc, s0, wts["delta_norm_g"], seq_len, chunk, rows, col["z"])

    if prompt:
        ob = _attn_prompt(qa, ka, va, wts["lam4"], wts["diff_norm_g"], seq_len, bk, lam_init)
    else:
        ob = _attn_cached(qa, ka, va, past_k.reshape(nb, past_k.shape[1], d),
                          past_v.reshape(nb, past_v.shape[1], d), wts["lam4"], wts["diff_norm_g"],
                          seq_len, lam_init)

    tmo = _pick(t, 256)
    h, xnt = _merge_out(x2, oa, ob, proj, wts["w_out"], wts["norm_ffn_g"], tmo, col["ga"], col["gb"])
    s0z, s1, tau = _peer_route(xnt, wts["wqt"], wts["sk"], _pick(t, 256))
    y = _peer_mix(xnt, wts["u"], wts["vt"], s0z, s1, tau, h, _pick(t, 512), 1024)

    new_k = kf.reshape(nb, seq_len, n_heads, 2, DIFF_QK)
    new_v = vf.reshape(nb, seq_len, n_heads, HEAD_W)
    assert seq_len >= CONV_W - 1
    new_conv = proj.reshape(nb, seq_len, proj.shape[1])[:, seq_len - (CONV_W - 1):, :qkv_w]
    return y.reshape(nb, seq_len, d), new_k, new_v, s_new, new_conv


def _prep_weights(l, d, norm_mix_g, w_in, conv_w, a_log, dt_bias, delta_norm_g, q_norm_g, k_norm_g,
                  lq1, lk1, lq2, lk2, diff_norm_g, w_out, norm_ffn_g, peer_w_q, peer_sub_keys, peer_u, peer_v):
    n_heads = d // HEAD_W
    qkv_w = 3 * d
    w = w_in[l]
    o_z = qkv_w
    o_beta = o_z + d
    o_a = o_beta + n_heads
    o_fq = o_a + n_heads
    w_main = jnp.concatenate([w[:, :o_beta], w[:, o_fq:]], axis=1).astype(BF16)
    w_ba = jnp.pad(w[:, o_beta:o_fq], ((0, 0), (0, LANES - 2 * n_heads))).astype(BF16)

    def lane_pad(v):
        return jnp.pad(v[l].astype(F32), (n_heads, LANES - 2 * n_heads)).reshape(1, LANES)

    return {
        "norm_mix_g": norm_mix_g[l].reshape(1, d),
        "w_main": w_main,
        "w_ba": w_ba,
        "conv_w": conv_w[l],
        "alog_pad": lane_pad(a_log),
        "dtb_pad": lane_pad(dt_bias),
        "delta_norm_g": delta_norm_g[l].reshape(1, HEAD_W),
        "gq2": jnp.tile(q_norm_g[l], 2).reshape(1, HEAD_W),
        "gk2": jnp.tile(k_norm_g[l], 2).reshape(1, HEAD_W),
        "lam4": jnp.stack([lq1[l], lk1[l], lq2[l], lk2[l]]).astype(F32),
        "diff_norm_g": diff_norm_g[l].reshape(1, HEAD_W),
        "w_out": w_out[l].astype(BF16),
        "norm_ffn_g": norm_ffn_g[l].reshape(1, d),
        "wqt": peer_w_q[l].T.astype(BF16),
        "sk": peer_sub_keys[l].astype(BF16),
        "u": peer_u[l].astype(BF16),
        "vt": peer_v[l].T.astype(BF16),
    }


def kernel(x_prompt, x_sample, cache_diff_k, cache_diff_v, state_delta_s, state_delta_conv, norm_mix_g, w_in, conv_w, delta_a_log, delta_dt_bias, delta_norm_g, diff_q_norm_g, diff_k_norm_g, diff_lambda_q1, diff_lambda_k1, diff_lambda_q2, diff_lambda_k2, diff_norm_g, w_out, norm_ffn_g, peer_w_q, peer_sub_keys, peer_u, peer_v):
    depth = w_in.shape[0]
    d = x_prompt.shape[-1]
    n_heads = d // HEAD_W
    yp, ys = x_prompt, x_sample
    outs_p, outs_s = [], []
    for l in range(depth):
        wts = _prep_weights(l, d, norm_mix_g, w_in, conv_w, delta_a_log, delta_dt_bias, delta_norm_g,
                            diff_q_norm_g, diff_k_norm_g, diff_lambda_q1, diff_lambda_k1, diff_lambda_q2,
                            diff_lambda_k2, diff_norm_g, w_out, norm_ffn_g, peer_w_q, peer_sub_keys,
                            peer_u, peer_v)
        lam_init = 0.8 - 0.6 * math.exp(-0.3 * l)
        conv0 = jnp.zeros((yp.shape[0], CONV_W - 1, 3 * d), F32)
        s0 = jnp.zeros((yp.shape[0], n_heads, HEAD_W, HEAD_W), F32)
        yp, *rest = _layer(yp, conv0, s0, None, None, lam_init, wts)
        outs_p.append(rest)
        ys, *rest = _layer(ys, state_delta_conv[l], state_delta_s[l], cache_diff_k[l], cache_diff_v[l],
                           lam_init, wts)
        outs_s.append(rest)
    stack = lambda outs, i: jnp.stack([o[i] for o in outs])
    return (yp, ys,
            stack(outs_p, 0), stack(outs_p, 1), stack(outs_p, 2), stack(outs_p, 3),
            stack(outs_s, 0), stack(outs_s, 1), stack(outs_s, 2), stack(outs_s, 3))
```

```python
import functools
import math

import jax
import jax.numpy as jnp
from jax import lax
from jax.experimental import pallas as pl
from jax.experimental.pallas import tpu as pltpu

F32 = jnp.float32
BF16 = jnp.bfloat16

RMS_EPS = 1e-6
NEG_INF = -1e30
LOG2_E = math.log2(math.e)
CHUNK = 64
HEAD_W = 128
N_HEADS = 8
DIFF_QK = 64
CONV_W = 4
PEER_HEADS = 8
N_KEYS = 128
PEER_TOPK = 16
LANES = 128
SUBLANES = 8
VMEM_LIMIT = 56 * 1024 * 1024


def _cparams(sem):
    return pltpu.CompilerParams(dimension_semantics=sem, vmem_limit_bytes=VMEM_LIMIT)


def _dot(a, b):
    return jnp.dot(a, b, preferred_element_type=F32)


def _dot_nt(a, b):
    return lax.dot_general(a, b, (((1,), (1,)), ((), ())), preferred_element_type=F32)


def _dot_tn(a, b):
    return lax.dot_general(a, b, (((0,), (0,)), ((), ())), preferred_element_type=F32)


def _sigmoid(x):
    return 1.0 / (1.0 + jnp.exp(-x))


def _silu(x):
    return x * _sigmoid(x)


def _in_proj_kernel(x_ref, g_ref, w_ref, wba_ref, o_ref, oba_ref, xn_ref):
    @pl.when(pl.program_id(1) == 0)
    def _():
        x = x_ref[...]
        r = lax.rsqrt(jnp.mean(x * x, axis=-1, keepdims=True) + RMS_EPS)
        xn = (x * r * g_ref[...]).astype(BF16)
        xn_ref[...] = xn
        oba_ref[...] = _dot(xn, wba_ref[...])

    o_ref[...] = _dot(xn_ref[...], w_ref[...])


def _in_proj(x, g, w_main, w_ba, tm, tn):
    t, d = x.shape
    nw = w_main.shape[1]
    return pl.pallas_call(
        _in_proj_kernel,
        grid=(t // tm, nw // tn),
        in_specs=[
            pl.BlockSpec((tm, d), lambda i, j: (i, 0)),
            pl.BlockSpec((1, d), lambda i, j: (0, 0)),
            pl.BlockSpec((d, tn), lambda i, j: (0, j)),
            pl.BlockSpec((d, LANES), lambda i, j: (0, 0)),
        ],
        out_specs=[
            pl.BlockSpec((tm, tn), lambda i, j: (i, j)),
            pl.BlockSpec((tm, LANES), lambda i, j: (i, 0)),
        ],
        out_shape=[jax.ShapeDtypeStruct((t, nw), F32), jax.ShapeDtypeStruct((t, LANES), F32)],
        scratch_shapes=[pltpu.VMEM((tm, d), BF16)],
        compiler_params=_cparams(("parallel", "arbitrary")),
        name="in_proj",
    )(x, g, w_main, w_ba)


def _delta_prep_kernel(cur_ref, prev_ref, cbuf_ref, cw_ref, ba_ref, alog_ref, dtb_ref,
                       q_ref, k_ref, v_ref, beta_ref, gc_ref, *, tiles_per_seq, chunk):
    tm = cur_ref.shape[0]
    qkv_w = cur_ref.shape[1]
    d = qkv_w // 3
    first = (pl.program_id(0) % tiles_per_seq) == 0
    col_w = 4 * HEAD_W
    for c0 in range(0, qkv_w, col_w):
        cur = cur_ref[:, c0:c0 + col_w]
        prev = jnp.where(first, cbuf_ref[:, c0:c0 + col_w], prev_ref[:, c0:c0 + col_w])
        ext = jnp.concatenate([prev, cur], axis=0)
        conv = cw_ref[0:1, c0:c0 + col_w] * ext[SUBLANES - 3:SUBLANES - 3 + tm]
        for j in range(1, CONV_W):
            conv = conv + cw_ref[j:j + 1, c0:c0 + col_w] * ext[SUBLANES - 3 + j:SUBLANES - 3 + j + tm]
        act = _silu(conv)
        for hh in range(col_w // HEAD_W):
            col = c0 + hh * HEAD_W
            a = act[:, hh * HEAD_W:(hh + 1) * HEAD_W]
            if col < 2 * d:
                a = a * lax.rsqrt(jnp.sum(a * a, axis=-1, keepdims=True) + RMS_EPS)
            if col < d:
                q_ref[:, col:col + HEAD_W] = (a * (HEAD_W ** -0.5)).astype(BF16)
            elif col < 2 * d:
                k_ref[:, col - d:col - d + HEAD_W] = a.astype(BF16)
            else:
                v_ref[:, col - 2 * d:col - 2 * d + HEAD_W] = a.astype(BF16)
    ba = ba_ref[...]
    beta_ref[...] = _sigmoid(ba)
    xs = ba + dtb_ref[...]
    softplus = jnp.maximum(xs, 0.0) + jnp.log(1.0 + jnp.exp(-jnp.abs(xs)))
    g = -jnp.exp(alog_ref[...]) * softplus
    row = lax.broadcasted_iota(jnp.int32, (tm, tm), 0)
    colm = lax.broadcasted_iota(jnp.int32, (tm, tm), 1)
    tri = jnp.where((row // chunk == colm // chunk) & (colm <= row), 1.0, 0.0).astype(F32)
    gc_ref[...] = jnp.dot(tri, g, preferred_element_type=F32, precision=lax.Precision.HIGHEST)


def _delta_prep(proj, proj_ba, cbuf8, conv_w, alog_pad, dtb_pad, seq_len, tm, chunk):
    t = proj.shape[0]
    qkv_w = conv_w.shape[1]
    d = qkv_w // 3
    tiles_per_seq = seq_len // tm
    rows8 = tm // SUBLANES
    kern = functools.partial(_delta_prep_kernel, tiles_per_seq=tiles_per_seq, chunk=chunk)
    return pl.pallas_call(
        kern,
        grid=(t // tm,),
        in_specs=[
            pl.BlockSpec((tm, qkv_w), lambda i: (i, 0)),
            pl.BlockSpec((SUBLANES, qkv_w), lambda i: (jnp.maximum(i * rows8 - 1, 0), 0)),
            pl.BlockSpec((None, SUBLANES, qkv_w), lambda i: (i // tiles_per_seq, 0, 0)),
            pl.BlockSpec((CONV_W, qkv_w), lambda i: (0, 0)),
            pl.BlockSpec((tm, LANES), lambda i: (i, 0)),
            pl.BlockSpec((1, LANES), lambda i: (0, 0)),
            pl.BlockSpec((1, LANES), lambda i: (0, 0)),
        ],
        out_specs=[
            pl.BlockSpec((tm, d), lambda i: (i, 0)),
            pl.BlockSpec((tm, d), lambda i: (i, 0)),
            pl.BlockSpec((tm, d), lambda i: (i, 0)),
            pl.BlockSpec((tm, LANES), lambda i: (i, 0)),
            pl.BlockSpec((tm, LANES), lambda i: (i, 0)),
        ],
        out_shape=[jax.ShapeDtypeStruct((t, d), BF16)] * 3 + [jax.ShapeDtypeStruct((t, LANES), F32)] * 2,
        compiler_params=_cparams(("parallel",)),
        name="delta_prep",
    )(proj, proj, cbuf8, conv_w, proj_ba, alog_pad, dtb_pad)


V_AUG = HEAD_W + 16


def _attn_prep_kernel(fq_ref, fk_ref, fv_ref, gq_ref, gk_ref, q_ref, kf_ref, ka_ref, v_ref, vf_ref, *, transposed):
    tm, d = fq_ref.shape
    vf_ref[...] = fv_ref[...]
    lane = lax.broadcasted_iota(jnp.int32, (tm, HEAD_W), 1)
    low = lane < DIFF_QK

    def qk_norm(x, g):
        s = x * x
        tot = jnp.sum(s, axis=-1, keepdims=True)
        lo = jnp.sum(jnp.where(low, s, 0.0), axis=-1, keepdims=True)
        ms = jnp.where(low, lo, tot - lo) * (1.0 / DIFF_QK)
        return x * lax.rsqrt(ms + RMS_EPS) * g

    for h in range(d // HEAD_W):
        sl = slice(h * HEAD_W, (h + 1) * HEAD_W)
        qn = qk_norm(fq_ref[:, sl], gq_ref[...]) * (DIFF_QK ** -0.5)
        kn = qk_norm(fk_ref[:, sl], gk_ref[...])
        kf_ref[:, sl] = kn
        if transposed:
            ka_ref[h] = kn.astype(BF16)
            q_ref[sl, :] = (qn * LOG2_E).T.astype(BF16)
            v_ref[h, 0:HEAD_W, :] = fv_ref[:, sl].T.astype(BF16)
            v_ref[h, HEAD_W:V_AUG, :] = jnp.ones((V_AUG - HEAD_W, tm), BF16)
        else:
            ka_ref[:, sl] = kn.astype(BF16)
            q_ref[:, sl] = qn.astype(BF16)
            v_ref[:, sl] = fv_ref[:, sl].astype(BF16)


def _attn_prep(proj, gq2, gk2, d, tm, col_q, col_k, col_v, seq_len, transposed):
    t = proj.shape[0]
    n_heads = d // HEAD_W
    row = pl.BlockSpec((tm, d), lambda i: (i, 0))
    if transposed:
        per_seq = seq_len // tm
        q_spec = pl.BlockSpec((d, tm), lambda i: (0, i))
        q_shape = jax.ShapeDtypeStruct((d, t), BF16)
        v_spec = pl.BlockSpec((None, n_heads, None, V_AUG, tm), lambda i: (i // per_seq, 0, i % per_seq, 0, 0))
        v_shape = jax.ShapeDtypeStruct((t // seq_len, n_heads, per_seq, V_AUG, tm), BF16)
        k_spec = pl.BlockSpec((None, n_heads, tm, HEAD_W), lambda i: (i // per_seq, 0, i % per_seq, 0))
        k_shape = jax.ShapeDtypeStruct((t // seq_len, n_heads, seq_len, HEAD_W), BF16)
    else:
        q_spec, q_shape = row, jax.ShapeDtypeStruct((t, d), BF16)
        v_spec, v_shape = row, jax.ShapeDtypeStruct((t, d), BF16)
        k_spec, k_shape = row, jax.ShapeDtypeStruct((t, d), BF16)
    return pl.pallas_call(
        functools.partial(_attn_prep_kernel, transposed=transposed),
        grid=(t // tm,),
        in_specs=[
            pl.BlockSpec((tm, d), lambda i: (i, col_q)),
            pl.BlockSpec((tm, d), lambda i: (i, col_k)),
            pl.BlockSpec((tm, d), lambda i: (i, col_v)),
            pl.BlockSpec((1, HEAD_W), lambda i: (0, 0)),
            pl.BlockSpec((1, HEAD_W), lambda i: (0, 0)),
        ],
        out_specs=[q_spec, row, k_spec, v_spec, row],
        out_shape=[q_shape, jax.ShapeDtypeStruct((t, d), F32), k_shape, v_shape,
                   jax.ShapeDtypeStruct((t, d), F32)],
        compiler_params=_cparams(("parallel",)),
        name="attn_prep",
    )(proj, proj, proj, gq2, gk2)


def _delta_rule_kernel(q_ref, k_ref, v_ref, z_ref, beta_ref, gc_ref, s0_ref, ng_ref,
                       o_ref, sfin_ref, s_ref, *, chunk, n_chunks):
    n = pl.program_id(1)
    rows = q_ref.shape[0]
    n_heads = q_ref.shape[1] // HEAD_W

    @pl.when(n == 0)
    def _():
        s_ref[...] = s0_ref[...]

    lane = lax.broadcasted_iota(jnp.int32, (rows, LANES), 1)
    ri = lax.broadcasted_iota(jnp.int32, (chunk, chunk), 0)
    ci = lax.broadcasted_iota(jnp.int32, (chunk, chunk), 1)
    eye = ri == ci
    incl = ci <= ri
    strict = ci < ri
    eye_f = jnp.where(eye, 1.0, 0.0).astype(F32)
    beta_all = beta_ref[...]
    gc_all = gc_ref[...]
    n_stage = int(math.log2(chunk))
    heads = range(n_heads)
    chunks = range(rows // chunk)
    items = [(h, c) for c in chunks for h in heads]

    beta_h = [jnp.sum(jnp.where(lane == h, beta_all, 0.0), axis=-1, keepdims=True) for h in heads]
    g_h = [jnp.sum(jnp.where(lane == n_heads + h, gc_all, 0.0), axis=-1, keepdims=True) for h in heads]
    q16, k16, kq, decay, eg, glast, gcol, rhs = {}, {}, {}, {}, {}, {}, {}, {}
    for it in items:
        h, c = it
        sl = slice(h * HEAD_W, (h + 1) * HEAD_W)
        rs = slice(c * chunk, (c + 1) * chunk)
        q16[it] = q_ref[rs, sl]
        k16[it] = k_ref[rs, sl]
        k = k16[it].astype(F32)
        beta = beta_h[h][rs]
        gcol[it] = g_h[h][rs]
        grow = jnp.sum(jnp.where(eye, gcol[it], 0.0), axis=0, keepdims=True)
        decay[it] = jnp.where(incl, jnp.exp(jnp.where(incl, gcol[it] - grow, 0.0)), 0.0)
        eg[it] = jnp.exp(gcol[it])
        glast[it] = gcol[it][chunk - 1:chunk]
        kb = k * beta
        kq[it] = jnp.concatenate([kb.astype(BF16), q16[it]], axis=0)
        rhs[it] = jnp.concatenate([(kb * eg[it]).astype(BF16),
                                   (v_ref[rs, sl].astype(F32) * beta).astype(BF16)], axis=1)
    kk = {it: _dot_nt(kq[it], k16[it]) for it in items}
    mp = {it: -jnp.where(strict, kk[it][0:chunk] * decay[it], 0.0) for it in items}
    tmat = {it: eye_f + mp[it] for it in items}
    for stage in range(1, n_stage):
        m16 = {it: mp[it].astype(BF16) for it in items}
        mp = {it: _dot(m16[it], m16[it]) for it in items}
        m16 = {it: mp[it].astype(BF16) for it in items}
        tmat = {it: tmat[it] + _dot(m16[it], tmat[it].astype(BF16)) for it in items}
    wu = {it: _dot(tmat[it].astype(BF16), rhs[it]) for it in items}
    intra16 = {it: (kk[it][chunk:2 * chunk] * decay[it]).astype(BF16) for it in items}

    s = [s_ref[h] for h in heads]
    for c in chunks:
        s16 = [s[h].astype(BF16) for h in heads]
        lhs = [jnp.concatenate([wu[(h, c)][:, 0:HEAD_W].astype(BF16),
                                (q16[(h, c)].astype(F32) * eg[(h, c)]).astype(BF16)], axis=0) for h in heads]
        ws = [_dot(lhs[h], s16[h]) for h in heads]
        vn16 = [(wu[(h, c)][:, HEAD_W:2 * HEAD_W] - ws[h][0:chunk]).astype(BF16) for h in heads]
        o = [ws[h][chunk:2 * chunk] + _dot(intra16[(h, c)], vn16[h]) for h in heads]
        kd16 = [(k16[(h, c)].astype(F32) * jnp.exp(glast[(h, c)] - gcol[(h, c)])).astype(BF16) for h in heads]
        s = [s[h] * jnp.exp(glast[(h, c)]) + _dot_tn(kd16[h], vn16[h]) for h in heads]
        for h in heads:
            sl = slice(h * HEAD_W, (h + 1) * HEAD_W)
            rs = slice(c * chunk, (c + 1) * chunk)
            on = o[h] * lax.rsqrt(jnp.mean(o[h] * o[h], axis=-1, keepdims=True) + RMS_EPS) * ng_ref[...]
            o_ref[rs, sl] = (on * _silu(z_ref[rs, sl])).astype(BF16)
    for h in heads:
        s_ref[h] = s[h]

    @pl.when(n == n_chunks - 1)
    def _():
        sfin_ref[...] = s_ref[...]


def _delta_rule(dq, dk, dv, proj, beta, gc, s0, ng, seq_len, chunk, rows, col_z):
    t, d = dq.shape
    nb = t // seq_len
    n_steps = seq_len // rows
    n_heads = d // HEAD_W
    kern = functools.partial(_delta_rule_kernel, chunk=chunk, n_chunks=n_steps)
    tok = lambda b, n: (b * n_steps + n, 0)
    return pl.pallas_call(
        kern,
        grid=(nb, n_steps),
        in_specs=[
            pl.BlockSpec((rows, d), tok),
            pl.BlockSpec((rows, d), tok),
            pl.BlockSpec((rows, d), tok),
            pl.BlockSpec((rows, d), lambda b, n: (b * n_steps + n, col_z)),
            pl.BlockSpec((rows, LANES), tok),
            pl.BlockSpec((rows, LANES), tok),
            pl.BlockSpec((None, n_heads, HEAD_W, HEAD_W), lambda b, n: (b, 0, 0, 0)),
            pl.BlockSpec((1, HEAD_W), lambda b, n: (0, 0)),
        ],
        out_specs=[
            pl.BlockSpec((rows, d), tok),
            pl.BlockSpec((None, n_heads, HEAD_W, HEAD_W), lambda b, n: (b, 0, 0, 0)),
        ],
        out_shape=[jax.ShapeDtypeStruct((t, d), BF16),
                   jax.ShapeDtypeStruct((nb, n_heads, HEAD_W, HEAD_W), F32)],
        scratch_shapes=[pltpu.VMEM((n_heads, HEAD_W, HEAD_W), F32)],
        compiler_params=_cparams(("parallel", "arbitrary")),
        name="delta_rule",
    )(dq, dk, dv, proj, beta, gc, s0, ng)


def _lambda(lam_ref, lam_init):
    l = lam_ref[...]
    a = jnp.sum(l[0:1] * l[1:2], axis=-1, keepdims=True)
    b = jnp.sum(l[2:3] * l[3:4], axis=-1, keepdims=True)
    return jnp.exp(a) - jnp.exp(b) + lam_init


def _sub_norm(o, g, lam_init):
    return o * lax.rsqrt(jnp.mean(o * o, axis=-1, keepdims=True) + RMS_EPS) * g * (1.0 - lam_init)


def _attn_prompt_kernel(qt_ref, k_ref, vt_ref, lam_ref, ng_ref, o_ref, m_ref, acc_ref, sa_ref, sb_ref, *, lam_init):
    i = pl.program_id(2)
    bq = qt_ref.shape[1]
    bk = vt_ref.shape[-1]
    assert bq == 2 * bk
    qt = qt_ref[...]
    row = lax.broadcasted_iota(jnp.int32, qt.shape, 0)
    zero = jnp.zeros_like(qt)
    qm = (jnp.where(row < DIFF_QK, qt, zero), jnp.where(row >= DIFF_QK, qt, zero))
    m_ref[...] = jnp.full(m_ref.shape, NEG_INF, F32)
    acc_ref[...] = jnp.zeros(acc_ref.shape, F32)

    def scores(j, s_ref):
        kb = k_ref[pl.ds(pl.multiple_of(j * bk, bk), bk), :]
        for m in range(2):
            s_ref[m] = _dot(kb, qm[m])

    def absorb(j, s_ref, mask):
        vb = vt_ref[j]
        for m in range(2):
            s = s_ref[m]
            if mask is not None:
                s = jnp.where(mask, s, NEG_INF)
            m_prev = m_ref[m]
            m_new = jnp.maximum(m_prev, jnp.max(s, axis=0, keepdims=True))
            alpha = jnp.exp2(m_prev - m_new)
            p = jnp.exp2(s - m_new).astype(BF16)
            acc_ref[m] = alpha * acc_ref[m] + _dot(vb, p)
            m_ref[m] = m_new

    scores(0, sa_ref)

    def body(jj, carry):
        scores(2 * jj + 1, sb_ref)
        absorb(2 * jj, sa_ref, None)
        scores(2 * jj + 2, sa_ref)
        absorb(2 * jj + 1, sb_ref, None)
        return carry

    lax.fori_loop(0, i, body, 0)
    scores(2 * i + 1, sb_ref)
    ri = lax.broadcasted_iota(jnp.int32, (bk, bq), 0)
    ci = lax.broadcasted_iota(jnp.int32, (bk, bq), 1)
    absorb(2 * i, sa_ref, (ri // CHUNK) <= (ci // CHUNK))
    absorb(2 * i + 1, sb_ref, ((ri + bk) // CHUNK) <= (ci // CHUNK))

    lam = _lambda(lam_ref, lam_init)
    a0 = acc_ref[0]
    a1 = acc_ref[1]
    ot = a0[0:HEAD_W] / a0[HEAD_W:HEAD_W + 1] - lam * (a1[0:HEAD_W] / a1[HEAD_W:HEAD_W + 1])
    o_ref[...] = _sub_norm(ot.T, ng_ref[...], lam_init).astype(BF16)


def _attn_prompt(qt, ka, vt, lam4, ng, seq_len, bk, lam_init):
    d, t = qt.shape
    nb = t // seq_len
    bq = 2 * bk
    nq = seq_len // bq
    nk = seq_len // bk
    n_heads = d // HEAD_W
    kern = functools.partial(_attn_prompt_kernel, lam_init=lam_init)
    return pl.pallas_call(
        kern,
        grid=(nb, n_heads, nq),
        in_specs=[
            pl.BlockSpec((HEAD_W, bq), lambda b, h, i: (h, b * nq + i)),
            pl.BlockSpec((None, None, seq_len, HEAD_W), lambda b, h, i: (b, h, 0, 0)),
            pl.BlockSpec((None, None, nk, V_AUG, bk), lambda b, h, i: (b, h, 0, 0, 0)),
            pl.BlockSpec((4, DIFF_QK), lambda b, h, i: (0, 0)),
            pl.BlockSpec((1, HEAD_W), lambda b, h, i: (0, 0)),
        ],
        out_specs=pl.BlockSpec((bq, HEAD_W), lambda b, h, i: (b * nq + i, h)),
        out_shape=jax.ShapeDtypeStruct((t, d), BF16),
        scratch_shapes=[pltpu.VMEM((2, 1, bq), F32), pltpu.VMEM((2, V_AUG, bq), F32),
                        pltpu.VMEM((2, bk, bq), F32), pltpu.VMEM((2, bk, bq), F32)],
        compiler_params=_cparams(("parallel", "parallel", "arbitrary")),
        name="attn_prompt",
    )(qt, ka, vt, lam4, ng)


def _attn_cached_kernel(q_ref, kn_ref, vn_ref, ck_ref, cv_ref, lam_ref, ng_ref, o_ref, *, lam_init):
    q = q_ref[...]
    lane = lax.broadcasted_iota(jnp.int32, q.shape, 1)
    zero = jnp.zeros_like(q)
    kp = ck_ref[...].astype(BF16)
    vp = cv_ref[...].astype(BF16)
    kn = kn_ref[...]
    vn = vn_ref[...]
    outs = []
    for m in range(2):
        qm = jnp.where((lane < DIFF_QK) if m == 0 else (lane >= DIFF_QK), q, zero)
        sp = _dot_nt(qm, kp)
        sn = _dot_nt(qm, kn)
        mx = jnp.maximum(jnp.max(sp, axis=-1, keepdims=True), jnp.max(sn, axis=-1, keepdims=True))
        pp = jnp.exp(sp - mx)
        pn = jnp.exp(sn - mx)
        den = jnp.sum(pp, axis=-1, keepdims=True) + jnp.sum(pn, axis=-1, keepdims=True)
        outs.append((_dot(pp.astype(BF16), vp) + _dot(pn.astype(BF16), vn)) / den)
    lam = _lambda(lam_ref, lam_init)
    o = outs[0] - lam * outs[1]
    o_ref[...] = _sub_norm(o, ng_ref[...], lam_init).astype(BF16)


def _attn_cached(qa, ka, va, cache_k, cache_v, lam4, ng, seq_len, lam_init):
    t, d = qa.shape
    nb = t // seq_len
    past = cache_k.shape[1]
    n_heads = d // HEAD_W
    kern = functools.partial(_attn_cached_kernel, lam_init=lam_init)
    tok = lambda b, h: (b, h)
    return pl.pallas_call(
        kern,
        grid=(nb, n_heads),
        in_specs=[
            pl.BlockSpec((seq_len, HEAD_W), tok),
            pl.BlockSpec((seq_len, HEAD_W), tok),
            pl.BlockSpec((seq_len, HEAD_W), tok),
            pl.BlockSpec((None, past, HEAD_W), lambda b, h: (b, 0, h)),
            pl.BlockSpec((None, past, HEAD_W), lambda b, h: (b, 0, h)),
            pl.BlockSpec((4, DIFF_QK), lambda b, h: (0, 0)),
            pl.BlockSpec((1, HEAD_W), lambda b, h: (0, 0)),
        ],
        out_specs=pl.BlockSpec((seq_len, HEAD_W), tok),
        out_shape=jax.ShapeDtypeStruct((t, d), BF16),
        compiler_params=_cparams(("parallel", "parallel")),
        name="attn_cached",
    )(qa, ka, va, cache_k, cache_v, lam4, ng)


def _merge_out_kernel(x_ref, oa_ref, ob_ref, ga_ref, gb_ref, wo_ref, g2_ref, h_ref, xnt_ref):
    merged = (_sigmoid(ga_ref[...]) * oa_ref[...].astype(F32)
              + _sigmoid(gb_ref[...]) * ob_ref[...].astype(F32))
    h = x_ref[...] + _dot(merged.astype(BF16), wo_ref[...])
    h_ref[...] = h
    hn = h * lax.rsqrt(jnp.mean(h * h, axis=-1, keepdims=True) + RMS_EPS) * g2_ref[...]
    xnt_ref[...] = hn.T.astype(BF16)


def _merge_out(x, oa, ob, proj, w_out, g2, tm, col_ga, col_gb):
    t, d = x.shape
    row = lambda i: (i, 0)
    return pl.pallas_call(
        _merge_out_kernel,
        grid=(t // tm,),
        in_specs=[
            pl.BlockSpec((tm, d), row),
            pl.BlockSpec((tm, d), row),
            pl.BlockSpec((tm, d), row),
            pl.BlockSpec((tm, d), lambda i: (i, col_ga)),
            pl.BlockSpec((tm, d), lambda i: (i, col_gb)),
            pl.BlockSpec((d, d), lambda i: (0, 0)),
            pl.BlockSpec((1, d), lambda i: (0, 0)),
        ],
        out_specs=[pl.BlockSpec((tm, d), row), pl.BlockSpec((d, tm), lambda i: (0, i))],
        out_shape=[jax.ShapeDtypeStruct((t, d), F32), jax.ShapeDtypeStruct((d, t), BF16)],
        compiler_params=_cparams(("parallel",)),
        name="merge_out",
    )(x, oa, ob, proj, proj, w_out, g2)


def _top_values(parts, count):
    vals = []
    for r in range(count):
        m = parts[0]
        for p in parts[1:]:
            m = jnp.maximum(m, p)
        m = jnp.max(m, axis=0, keepdims=True)
        vals.append(m)
        if r + 1 < count:
            parts = [jnp.where(p == m, NEG_INF, p) for p in parts]
    return vals


def _sorting_network(n):
    pairs = []
    p = 1
    while p < n:
        k = p
        while k >= 1:
            for j in range(k % p, n - k, 2 * k):
                for i in range(min(k, n - j - k)):
                    if (i + j) // (2 * p) == (i + j + k) // (2 * p):
                        pairs.append((i + j, i + j + k))
            k //= 2
        p *= 2
    return pairs


def _top_values_sorted(x, count):
    m = x.shape[0] // SUBLANES
    assert count <= m
    tiles = [x[SUBLANES * v:SUBLANES * (v + 1)] for v in range(m)]
    for i, j in _sorting_network(m):
        tiles[i], tiles[j] = jnp.maximum(tiles[i], tiles[j]), jnp.minimum(tiles[i], tiles[j])
    sub = lax.broadcasted_iota(jnp.int32, tiles[0].shape, 0).astype(F32)
    vals = []
    for r in range(count):
        top = jnp.max(tiles[0], axis=0, keepdims=True)
        vals.append(top)
        live = count - r - 1
        if live > 0:
            first = jnp.min(jnp.where(tiles[0] == top, sub, float(SUBLANES)), axis=0, keepdims=True)
            pop = sub == first
            for v in range(live):
                tiles[v] = jnp.where(pop, tiles[v + 1], tiles[v])
    return vals


def _pair_candidates(sv0, sv1, count):
    k = sv0.shape[0]
    parts = []
    first_single = k
    for i in range(k):
        nj = min(k, count // (i + 1))
        if nj > 1:
            for j0 in range(0, nj, SUBLANES):
                parts.append(sv0[i:i + 1] + sv1[j0:j0 + SUBLANES])
        else:
            first_single = min(first_single, i)
    for i0 in range(first_single, k, SUBLANES):
        parts.append(sv0[i0:i0 + SUBLANES] + sv1[0:1])
    return parts


def _peer_route_kernel(xnt_ref, wqt_ref, sk_ref, s0_ref, s1_ref, tau_ref):
    pq = _dot(wqt_ref[...], xnt_ref[...]).astype(BF16)
    dh = sk_ref.shape[-1]
    taus = []
    for h in range(PEER_HEADS):
        sc = []
        for p in range(2):
            r0 = (h * 2 + p) * dh
            sc.append(_dot(sk_ref[h, p], pq[r0:r0 + dh]))
        sv0 = jnp.concatenate(_top_values_sorted(sc[0], PEER_TOPK), axis=0)
        sv1 = jnp.concatenate(_top_values_sorted(sc[1], PEER_TOPK), axis=0)
        tv = _top_values(_pair_candidates(sv0, sv1, PEER_TOPK + 1), PEER_TOPK + 1)
        mx = tv[0]
        z = jnp.exp(tv[0] - mx)
        for r in range(1, PEER_TOPK):
            z = z + jnp.exp(tv[r] - mx)
        logz = jnp.log(z)
        taus.append((0.5 * (tv[PEER_TOPK - 1] + tv[PEER_TOPK]) - mx - logz) * LOG2_E)
        s0_ref[h] = (sc[0] - sv0[0:1] - logz) * LOG2_E
        s1_ref[h] = (sc[1] - sv1[0:1]) * LOG2_E
    tau_ref[...] = jnp.concatenate(taus, axis=0)


def _peer_route(xnt, wqt, sk16, tm):
    d, t = xnt.shape
    qw = wqt.shape[0]
    tok = lambda i: (0, 0, i)
    return pl.pallas_call(
        _peer_route_kernel,
        grid=(t // tm,),
        in_specs=[
            pl.BlockSpec((d, tm), lambda i: (0, i)),
            pl.BlockSpec((qw, d), lambda i: (0, 0)),
            pl.BlockSpec(sk16.shape, lambda i: (0, 0, 0, 0)),
        ],
        out_specs=[
            pl.BlockSpec((PEER_HEADS, N_KEYS, tm), tok),
            pl.BlockSpec((PEER_HEADS, N_KEYS, tm), tok),
            pl.BlockSpec((PEER_HEADS, tm), lambda i: (0, i)),
        ],
        out_shape=[jax.ShapeDtypeStruct((PEER_HEADS, N_KEYS, t), F32),
                   jax.ShapeDtypeStruct((PEER_HEADS, N_KEYS, t), F32),
                   jax.ShapeDtypeStruct((PEER_HEADS, t), F32)],
        compiler_params=_cparams(("parallel",)),
        name="peer_route",
    )(xnt, wqt, sk16)


def _gelu(x):
    return 0.5 * x * (1.0 + lax.erf(x * (2.0 ** -0.5)))


PEER_EXPERT_BLOCK = 1024
GROUP_ROWS = 64


BF16_ROWS = 16


def _peer_mix_kernel(xnt_ref, u_ref, vt_ref, s0_ref, s1_ref, tau_ref, h_ref, y_ref, acc_ref, e1_ref, *, n_eblk):
    j = pl.program_id(1)
    na = s0_ref.shape[1]
    tm = xnt_ref.shape[1]

    @pl.when(j == 0)
    def _():
        acc_ref[...] = jnp.zeros(acc_ref.shape, F32)
        e1_ref[...] = jnp.exp2(s1_ref[...]).astype(BF16)

    xnt = xnt_ref[...]
    sub = 2 * N_KEYS
    n_sub = na * N_KEYS // sub
    lw = min(tm, LANES)
    grp = GROUP_ROWS // BF16_ROWS

    def activations(k):
        return _dot(u_ref[k * sub:(k + 1) * sub, :], xnt)

    def row16(x):
        return jnp.broadcast_to(x, (BF16_ROWS, lw)).astype(BF16)[None]

    act = activations(0)
    out = None
    for k in range(n_sub):
        act_next = activations(k + 1) if k + 1 < n_sub else None
        a_lo, a_hi = k * sub // N_KEYS, (k + 1) * sub // N_KEYS
        piece = {}
        for t0 in range(0, tm, lw):
            ts = slice(t0, t0 + lw)
            s0 = [s0_ref[h, a_lo:a_hi, ts] for h in range(PEER_HEADS)]
            e0 = [jnp.exp2(s0[h]) for h in range(PEER_HEADS)]
            e1_min = [jnp.exp2(tau_ref[h:h + 1, ts] - s0[h]) for h in range(PEER_HEADS)]
            for a in range(a_lo, a_hi):
                ar = slice(a - a_lo, a - a_lo + 1)
                rows = [(row16(e1_min[h][ar]), row16(e0[h][ar])) for h in range(PEER_HEADS)]
                for b0 in range(0, N_KEYS, GROUP_ROWS):
                    r0 = a * N_KEYS + b0 - k * sub
                    wgt = jnp.zeros((grp, BF16_ROWS, lw), BF16)
                    for h in range(PEER_HEADS):
                        e1 = e1_ref[h, b0:b0 + GROUP_ROWS, ts].reshape(grp, BF16_ROWS, lw)
                        wgt = wgt + jnp.where(e1 >= rows[h][0], e1, jnp.zeros_like(e1)) * rows[h][1]
                    gate = _gelu(act[r0:r0 + GROUP_ROWS, ts].astype(BF16))
                    piece[(r0, t0)] = wgt.reshape(GROUP_ROWS, lw) * gate
        gw = jnp.concatenate([jnp.concatenate([piece[(r0, t0)] for t0 in range(0, tm, lw)], axis=1)
                              for r0 in range(0, sub, GROUP_ROWS)], axis=0)
        part = _dot(vt_ref[:, k * sub:(k + 1) * sub], gw)
        out = part if out is None else out + part
        act = act_next
    acc_ref[...] += out

    @pl.when(j == n_eblk - 1)
    def _():
        y_ref[...] = h_ref[...] + acc_ref[...].T


def _peer_mix(xnt, u16, vt16, s0z, s1, tau, h, tm):
    d, t = xnt.shape
    n_exp = u16.shape[0]
    te = vt16.shape[2]
    na = te // N_KEYS
    n_eblk = n_exp // te
    kern = functools.partial(_peer_mix_kernel, n_eblk=n_eblk)
    return pl.pallas_call(
        kern,
        grid=(t // tm, n_eblk),
        in_specs=[
            pl.BlockSpec((d, tm), lambda i, j: (0, i)),
            pl.BlockSpec((te, d), lambda i, j: (j, 0)),
            pl.BlockSpec((None, d, te), lambda i, j: (j, 0, 0)),
            pl.BlockSpec((PEER_HEADS, na, tm), lambda i, j: (0, j, i)),
            pl.BlockSpec((PEER_HEADS, N_KEYS, tm), lambda i, j: (0, 0, i)),
            pl.BlockSpec((PEER_HEADS, tm), lambda i, j: (0, i)),
            pl.BlockSpec((tm, d), lambda i, j: (i, 0)),
        ],
        out_specs=pl.BlockSpec((tm, d), lambda i, j: (i, 0)),
        out_shape=jax.ShapeDtypeStruct((t, d), F32),
        scratch_shapes=[pltpu.VMEM((d, tm), F32), pltpu.VMEM((PEER_HEADS, N_KEYS, tm), BF16)],
        compiler_params=_cparams(("parallel", "arbitrary")),
        name="peer_mix",
    )(xnt, u16, vt16, s0z, s1, tau, h)


def _pick(n, pref):
    t = min(n, pref)
    while n % t:
        t //= 2
    return t


def _layer(x, conv_buf, s0, past_k, past_v, lam_init, wts):
    nb, seq_len, d = x.shape
    t = nb * seq_len
    n_heads = d // HEAD_W
    qkv_w = 3 * d
    x2 = x.reshape(t, d)

    tm = _pick(t, 2048)
    proj, proj_ba = _in_proj(x2, wts["norm_mix_g"], wts["w_main"], wts["w_ba"], tm, 1024)
    col ={name: idx for idx, name in enumerate(("q", "k", "v", "z", "fq", "fk", "fv", "ga", "gb"))}

    chunk = min(CHUNK, seq_len)
    tp = _pick(seq_len, 256)
    cbuf8 = jnp.concatenate(
        [jnp.zeros((nb, SUBLANES - (CONV_W - 1), qkv_w), F32), conv_buf.astype(F32)], axis=1)
    dq, dk, dv, beta, gc = _delta_prep(proj, proj_ba, cbuf8, wts["conv_w"], wts["alog_pad"], wts["dtb_pad"],
                                       seq_len, tp, chunk)
    prompt = past_k is None
    bk = _pick(seq_len // 2, 512)
    qa, kf, ka, va, vf = _attn_prep(proj, wts["gq2"], wts["gk2"], d, bk if prompt else _pick(t, 256),
                                col["fq"], col["fk"], col["fv"], seq_len, prompt)

    rows = min(seq_len, 4 * chunk)
    oa, s_new = _delta_rule(dq, dk, dv, proj, beta, gc, s0, wts["delta_norm_g"], seq_len, chunk, rows, col["z"])

    if prompt:
        ob = _attn_prompt(qa, ka, va, wts["lam4"], wts["diff_norm_g"], seq_len, bk, lam_init)
    else:
        ob = _attn_cached(qa, ka, va, past_k.reshape(nb, past_k.shape[1], d),
                          past_v.reshape(nb, past_v.shape[1], d), wts["lam4"], wts["diff_norm_g"],
                          seq_len, lam_init)

    tmo = _pick(t, 256)
    h, xnt = _merge_out(x2, oa, ob, proj, wts["w_out"], wts["norm_ffn_g"], tmo, col["ga"], col["gb"])
    s0z, s1, tau = _peer_route(xnt, wts["wqt"], wts["sk"], _pick(t, 256))
    y = _peer_mix(xnt, wts["u"], wts["vt"], s0z, s1, tau, h, _pick(t, 512))

    new_k = kf.reshape(nb, seq_len, n_heads, 2, DIFF_QK)
    new_v = vf.reshape(nb, seq_len, n_heads, HEAD_W)
    assert seq_len >= CONV_W - 1
    new_conv = proj.reshape(nb, seq_len, proj.shape[1])[:, seq_len - (CONV_W - 1):, :qkv_w]
    return y.reshape(nb, seq_len, d), new_k, new_v, s_new, new_conv


def _prep_weights(l, d, norm_mix_g, w_in, conv_w, a_log, dt_bias, delta_norm_g, q_norm_g, k_norm_g,
                  lq1, lk1, lq2, lk2, diff_norm_g, w_out, norm_ffn_g, peer_w_q, peer_sub_keys, peer_u, peer_v):
    n_heads = d // HEAD_W
    qkv_w = 3 * d
    w = w_in[l]
    o_z = qkv_w
    o_beta = o_z + d
    o_a = o_beta + n_heads
    o_fq = o_a + n_heads
    w_main = jnp.concatenate([w[:, :o_beta], w[:, o_fq:]], axis=1).astype(BF16)
    w_ba = jnp.pad(w[:, o_beta:o_fq], ((0, 0), (0, LANES - 2 * n_heads))).astype(BF16)

    def lane_pad(v):
        return jnp.pad(v[l].astype(F32), (n_heads, LANES - 2 * n_heads)).reshape(1, LANES)

    return {
        "norm_mix_g": norm_mix_g[l].reshape(1, d),
        "w_main": w_main,
        "w_ba": w_ba,
        "conv_w": conv_w[l],
        "alog_pad": lane_pad(a_log),
        "dtb_pad": lane_pad(dt_bias),
        "delta_norm_g": delta_norm_g[l].reshape(1, HEAD_W),
        "gq2": jnp.tile(q_norm_g[l], 2).reshape(1, HEAD_W),
        "gk2": jnp.tile(k_norm_g[l], 2).reshape(1, HEAD_W),
        "lam4": jnp.stack([lq1[l], lk1[l], lq2[l], lk2[l]]).astype(F32),
        "diff_norm_g": diff_norm_g[l].reshape(1, HEAD_W),
        "w_out": w_out[l].astype(BF16),
        "norm_ffn_g": norm_ffn_g[l].reshape(1, d),
        "wqt": peer_w_q[l].T.astype(BF16),
        "sk": peer_sub_keys[l].astype(BF16),
        "u": peer_u[l].astype(BF16),
        "vt": peer_v[l].reshape(-1, PEER_EXPERT_BLOCK, d).transpose(0, 2, 1).astype(BF16),
    }


def kernel(x_prompt, x_sample, cache_diff_k, cache_diff_v, state_delta_s, state_delta_conv, norm_mix_g, w_in, conv_w, delta_a_log, delta_dt_bias, delta_norm_g, diff_q_norm_g, diff_k_norm_g, diff_lambda_q1, diff_lambda_k1, diff_lambda_q2, diff_lambda_k2, diff_norm_g, w_out, norm_ffn_g, peer_w_q, peer_sub_keys, peer_u, peer_v):
    depth = w_in.shape[0]
    d = x_prompt.shape[-1]
    n_heads = d // HEAD_W
    yp, ys = x_prompt, x_sample
    outs_p, outs_s = [], []
    for l in range(depth):
        wts = _prep_weights(l, d, norm_mix_g, w_in, conv_w, delta_a_log, delta_dt_bias, delta_norm_g,
                            diff_q_norm_g, diff_k_norm_g, diff_lambda_q1, diff_lambda_k1, diff_lambda_q2,
                            diff_lambda_k2, diff_norm_g, w_out, norm_ffn_g, peer_w_q, peer_sub_keys,
                            peer_u, peer_v)
        lam_init = 0.8 - 0.6 * math.exp(-0.3 * l)
        conv0 = jnp.zeros((yp.shape[0], CONV_W - 1, 3 * d), F32)
        s0 = jnp.zeros((yp.shape[0], n_heads, HEAD_W, HEAD_W), F32)
        yp, *rest = _layer(yp, conv0, s0, None, None, lam_init, wts)
        outs_p.append(rest)
        ys, *rest = _layer(ys, state_delta_conv[l], state_delta_s[l], cache_diff_k[l], cache_diff_v[l],
                           lam_init, wts)
        outs_s.append(rest)
    stack = lambda outs, i: jnp.stack([o[i] for o in outs])
    return (yp, ys,
            stack(outs_p, 0), stack(outs_p, 1), stack(outs_p, 2), stack(outs_p, 3),
            stack(outs_s, 0), stack(outs_s, 1), stack(outs_s, 2), stack(outs_s, 3))
```

```python
import functools
import math

import jax
import jax.numpy as jnp
from jax import lax
from jax.experimental import pallas as pl
from jax.experimental.pallas import tpu as pltpu

F32 = jnp.float32
BF16 = jnp.bfloat16

RMS_EPS = 1e-6
NEG_INF = -1e30
LOG2_E = math.log2(math.e)
CHUNK = 64
HEAD_W = 128
N_HEADS = 8
DIFF_QK = 64
CONV_W = 4
PEER_HEADS = 8
N_KEYS = 128
PEER_TOPK = 16
LANES = 128
SUBLANES = 8
VMEM_LIMIT = 56 * 1024 * 1024


def _cparams(sem):
    return pltpu.CompilerParams(dimension_semantics=sem, vmem_limit_bytes=VMEM_LIMIT)


def _dot(a, b):
    return jnp.dot(a, b, preferred_element_type=F32)


def _dot_nt(a, b):
    return lax.dot_general(a, b, (((1,), (1,)), ((), ())), preferred_element_type=F32)


def _dot_tn(a, b):
    return lax.dot_general(a, b, (((0,), (0,)), ((), ())), preferred_element_type=F32)


def _sigmoid(x):
    return 1.0 / (1.0 + jnp.exp(-x))


def _silu(x):
    return x * _sigmoid(x)


def _in_proj_kernel(x_ref, g_ref, w_ref, wba_ref, o_ref, oba_ref, xn_ref):
    @pl.when(pl.program_id(1) == 0)
    def _():
        x = x_ref[...]
        r = lax.rsqrt(jnp.mean(x * x, axis=-1, keepdims=True) + RMS_EPS)
        xn = (x * r * g_ref[...]).astype(BF16)
        xn_ref[...] = xn
        oba_ref[...] = _dot(xn, wba_ref[...])

    o_ref[...] = _dot(xn_ref[...], w_ref[...])


def _in_proj(x, g, w_main, w_ba, tm, tn):
    t, d = x.shape
    nw = w_main.shape[1]
    return pl.pallas_call(
        _in_proj_kernel,
        grid=(t // tm, nw // tn),
        in_specs=[
            pl.BlockSpec((tm, d), lambda i, j: (i, 0)),
            pl.BlockSpec((1, d), lambda i, j: (0, 0)),
            pl.BlockSpec((d, tn), lambda i, j: (0, j)),
            pl.BlockSpec((d, LANES), lambda i, j: (0, 0)),
        ],
        out_specs=[
            pl.BlockSpec((tm, tn), lambda i, j: (i, j)),
            pl.BlockSpec((tm, LANES), lambda i, j: (i, 0)),
        ],
        out_shape=[jax.ShapeDtypeStruct((t, nw), F32), jax.ShapeDtypeStruct((t, LANES), F32)],
        scratch_shapes=[pltpu.VMEM((tm, d), BF16)],
        compiler_params=_cparams(("parallel", "arbitrary")),
        name="in_proj",
    )(x, g, w_main, w_ba)


def _delta_prep_kernel(cur_ref, prev_ref, cbuf_ref, cw_ref, ba_ref, alog_ref, dtb_ref,
                       q_ref, k_ref, v_ref, beta_ref, gc_ref, *, tiles_per_seq, chunk):
    tm = cur_ref.shape[0]
    qkv_w = cur_ref.shape[1]
    d = qkv_w // 3
    first = (pl.program_id(0) % tiles_per_seq) == 0
    col_w = 4 * HEAD_W
    for c0 in range(0, qkv_w, col_w):
        cur = cur_ref[:, c0:c0 + col_w]
        prev = jnp.where(first, cbuf_ref[:, c0:c0 + col_w], prev_ref[:, c0:c0 + col_w])
        ext = jnp.concatenate([prev, cur], axis=0)
        conv = cw_ref[0:1, c0:c0 + col_w] * ext[SUBLANES - 3:SUBLANES - 3 + tm]
        for j in range(1, CONV_W):
            conv = conv + cw_ref[j:j + 1, c0:c0 + col_w] * ext[SUBLANES - 3 + j:SUBLANES - 3 + j + tm]
        act = _silu(conv)
        for hh in range(col_w // HEAD_W):
            col = c0 + hh * HEAD_W
            a = act[:, hh * HEAD_W:(hh + 1) * HEAD_W]
            if col < 2 * d:
                a = a * lax.rsqrt(jnp.sum(a * a, axis=-1, keepdims=True) + RMS_EPS)
            if col < d:
                q_ref[:, col:col + HEAD_W] = (a * (HEAD_W ** -0.5)).astype(BF16)
            elif col < 2 * d:
                k_ref[:, col - d:col - d + HEAD_W] = a.astype(BF16)
            else:
                v_ref[:, col - 2 * d:col - 2 * d + HEAD_W] = a.astype(BF16)
    ba = ba_ref[...]
    beta_ref[...] = _sigmoid(ba)
    xs = ba + dtb_ref[...]
    softplus = jnp.maximum(xs, 0.0) + jnp.log(1.0 + jnp.exp(-jnp.abs(xs)))
    g = -jnp.exp(alog_ref[...]) * softplus
    row = lax.broadcasted_iota(jnp.int32, (tm, tm), 0)
    colm = lax.broadcasted_iota(jnp.int32, (tm, tm), 1)
    tri = jnp.where((row // chunk == colm // chunk) & (colm <= row), 1.0, 0.0).astype(F32)
    gc_ref[...] = jnp.dot(tri, g, preferred_element_type=F32, precision=lax.Precision.HIGHEST)


def _delta_prep(proj, proj_ba, cbuf8, conv_w, alog_pad, dtb_pad, seq_len, tm, chunk):
    t = proj.shape[0]
    qkv_w = conv_w.shape[1]
    d = qkv_w // 3
    tiles_per_seq = seq_len // tm
    rows8 = tm // SUBLANES
    kern = functools.partial(_delta_prep_kernel, tiles_per_seq=tiles_per_seq, chunk=chunk)
    return pl.pallas_call(
        kern,
        grid=(t // tm,),
        in_specs=[
            pl.BlockSpec((tm, qkv_w), lambda i: (i, 0)),
            pl.BlockSpec((SUBLANES, qkv_w), lambda i: (jnp.maximum(i * rows8 - 1, 0), 0)),
            pl.BlockSpec((None, SUBLANES, qkv_w), lambda i: (i // tiles_per_seq, 0, 0)),
            pl.BlockSpec((CONV_W, qkv_w), lambda i: (0, 0)),
            pl.BlockSpec((tm, LANES), lambda i: (i, 0)),
            pl.BlockSpec((1, LANES), lambda i: (0, 0)),
            pl.BlockSpec((1, LANES), lambda i: (0, 0)),
        ],
        out_specs=[
            pl.BlockSpec((tm, d), lambda i: (i, 0)),
            pl.BlockSpec((tm, d), lambda i: (i, 0)),
            pl.BlockSpec((tm, d), lambda i: (i, 0)),
            pl.BlockSpec((tm, LANES), lambda i: (i, 0)),
            pl.BlockSpec((tm, LANES), lambda i: (i, 0)),
        ],
        out_shape=[jax.ShapeDtypeStruct((t, d), BF16)] * 3 + [jax.ShapeDtypeStruct((t, LANES), F32)] * 2,
        compiler_params=_cparams(("parallel",)),
        name="delta_prep",
    )(proj, proj, cbuf8, conv_w, proj_ba, alog_pad, dtb_pad)


V_AUG = HEAD_W + 16


def _attn_prep_kernel(fq_ref, fk_ref, fv_ref, gq_ref, gk_ref, q_ref, kf_ref, ka_ref, v_ref, vf_ref, *, transposed):
    tm, d = fq_ref.shape
    vf_ref[...] = fv_ref[...]
    lane = lax.broadcasted_iota(jnp.int32, (tm, HEAD_W), 1)
    low = lane < DIFF_QK

    def qk_norm(x, g):
        s = x * x
        tot = jnp.sum(s, axis=-1, keepdims=True)
        lo = jnp.sum(jnp.where(low, s, 0.0), axis=-1, keepdims=True)
        ms = jnp.where(low, lo, tot - lo) * (1.0 / DIFF_QK)
        return x * lax.rsqrt(ms + RMS_EPS) * g

    for h in range(d // HEAD_W):
        sl = slice(h * HEAD_W, (h + 1) * HEAD_W)
        qn = qk_norm(fq_ref[:, sl], gq_ref[...]) * (DIFF_QK ** -0.5)
        kn = qk_norm(fk_ref[:, sl], gk_ref[...])
        kf_ref[:, sl] = kn
        if transposed:
            ka_ref[h] = kn.astype(BF16)
            q_ref[sl, :] = (qn * LOG2_E).T.astype(BF16)
            v_ref[h, 0:HEAD_W, :] = fv_ref[:, sl].T.astype(BF16)
            v_ref[h, HEAD_W:V_AUG, :] = jnp.ones((V_AUG - HEAD_W, tm), BF16)
        else:
            ka_ref[:, sl] = kn.astype(BF16)
            q_ref[:, sl] = qn.astype(BF16)
            v_ref[:, sl] = fv_ref[:, sl].astype(BF16)


def _attn_prep(proj, gq2, gk2, d, tm, col_q, col_k, col_v, seq_len, transposed):
    t = proj.shape[0]
    n_heads = d // HEAD_W
    row = pl.BlockSpec((tm, d), lambda i: (i, 0))
    if transposed:
        per_seq = seq_len // tm
        q_spec = pl.BlockSpec((d, tm), lambda i: (0, i))
        q_shape = jax.ShapeDtypeStruct((d, t), BF16)
        v_spec = pl.BlockSpec((None, n_heads, None, V_AUG, tm), lambda i: (i // per_seq, 0, i % per_seq, 0, 0))
        v_shape = jax.ShapeDtypeStruct((t // seq_len, n_heads, per_seq, V_AUG, tm), BF16)
        k_spec = pl.BlockSpec((None, n_heads, tm, HEAD_W), lambda i: (i // per_seq, 0, i % per_seq, 0))
        k_shape = jax.ShapeDtypeStruct((t // seq_len, n_heads, seq_len, HEAD_W), BF16)
    else:
        q_spec, q_shape = row, jax.ShapeDtypeStruct((t, d), BF16)
        v_spec, v_shape = row, jax.ShapeDtypeStruct((t, d), BF16)
        k_spec, k_shape = row, jax.ShapeDtypeStruct((t, d), BF16)
    return pl.pallas_call(
        functools.partial(_attn_prep_kernel, transposed=transposed),
        grid=(t // tm,),
        in_specs=[
            pl.BlockSpec((tm, d), lambda i: (i, col_q)),
            pl.BlockSpec((tm, d), lambda i: (i, col_k)),
            pl.BlockSpec((tm, d), lambda i: (i, col_v)),
            pl.BlockSpec((1, HEAD_W), lambda i: (0, 0)),
            pl.BlockSpec((1, HEAD_W), lambda i: (0, 0)),
        ],
        out_specs=[q_spec, row, k_spec, v_spec, row],
        out_shape=[q_shape, jax.ShapeDtypeStruct((t, d), F32), k_shape, v_shape,
                   jax.ShapeDtypeStruct((t, d), F32)],
        compiler_params=_cparams(("parallel",)),
        name="attn_prep",
    )(proj, proj, proj, gq2, gk2)


def _delta_rule_kernel(q_ref, k_ref, v_ref, z_ref, beta_ref, gc_ref, s0_ref, ng_ref,
                       o_ref, sfin_ref, s_ref, *, chunk, n_chunks):
    n = pl.program_id(1)
    rows = q_ref.shape[0]
    n_heads = q_ref.shape[1] // HEAD_W

    @pl.when(n == 0)
    def _():
        s_ref[...] = s0_ref[...]

    lane = lax.broadcasted_iota(jnp.int32, (rows, LANES), 1)
    ri = lax.broadcasted_iota(jnp.int32, (chunk, chunk), 0)
    ci = lax.broadcasted_iota(jnp.int32, (chunk, chunk), 1)
    eye = ri == ci
    incl = ci <= ri
    strict = ci < ri
    eye_f = jnp.where(eye, 1.0, 0.0).astype(F32)
    beta_all = beta_ref[...]
    gc_all = gc_ref[...]
    n_stage = int(math.log2(chunk))
    heads = range(n_heads)
    chunks = range(rows // chunk)
    items = [(h, c) for c in chunks for h in heads]

    beta_h = [jnp.sum(jnp.where(lane == h, beta_all, 0.0), axis=-1, keepdims=True) for h in heads]
    g_h = [jnp.sum(jnp.where(lane == n_heads + h, gc_all, 0.0), axis=-1, keepdims=True) for h in heads]
    q16, k16, kq, decay, eg, glast, gcol, rhs = {}, {}, {}, {}, {}, {}, {}, {}
    for it in items:
        h, c = it
        sl = slice(h * HEAD_W, (h + 1) * HEAD_W)
        rs = slice(c * chunk, (c + 1) * chunk)
        q16[it] = q_ref[rs, sl]
        k16[it] = k_ref[rs, sl]
        k = k16[it].astype(F32)
        beta = beta_h[h][rs]
        gcol[it] = g_h[h][rs]
        grow = jnp.sum(jnp.where(eye, gcol[it], 0.0), axis=0, keepdims=True)
        decay[it] = jnp.where(incl, jnp.exp(jnp.where(incl, gcol[it] - grow, 0.0)), 0.0)
        eg[it] = jnp.exp(gcol[it])
        glast[it] = gcol[it][chunk - 1:chunk]
        kb = k * beta
        kq[it] = jnp.concatenate([kb.astype(BF16), q16[it]], axis=0)
        rhs[it] = jnp.concatenate([(kb * eg[it]).astype(BF16),
                                   (v_ref[rs, sl].astype(F32) * beta).astype(BF16)], axis=1)
    kk = {it: _dot_nt(kq[it], k16[it]) for it in items}
    mp = {it: -jnp.where(strict, kk[it][0:chunk] * decay[it], 0.0) for it in items}
    tmat = {it: eye_f + mp[it] for it in items}
    for stage in range(1, n_stage):
        m16 = {it: mp[it].astype(BF16) for it in items}
        mp = {it: _dot(m16[it], m16[it]) for it in items}
        m16 = {it: mp[it].astype(BF16) for it in items}
        tmat = {it: tmat[it] + _dot(m16[it], tmat[it].astype(BF16)) for it in items}
    wu = {it: _dot(tmat[it].astype(BF16), rhs[it]) for it in items}
    intra16 = {it: (kk[it][chunk:2 * chunk] * decay[it]).astype(BF16) for it in items}

    s = [s_ref[h] for h in heads]
    for c in chunks:
        s16 = [s[h].astype(BF16) for h in heads]
        lhs = [jnp.concatenate([wu[(h, c)][:, 0:HEAD_W].astype(BF16),
                                (q16[(h, c)].astype(F32) * eg[(h, c)]).astype(BF16)], axis=0) for h in heads]
        ws = [_dot(lhs[h], s16[h]) for h in heads]
        vn16 = [(wu[(h, c)][:, HEAD_W:2 * HEAD_W] - ws[h][0:chunk]).astype(BF16) for h in heads]
        o = [ws[h][chunk:2 * chunk] + _dot(intra16[(h, c)], vn16[h]) for h in heads]
        kd16 = [(k16[(h, c)].astype(F32) * jnp.exp(glast[(h, c)] - gcol[(h, c)])).astype(BF16) for h in heads]
        s = [s[h] * jnp.exp(glast[(h, c)]) + _dot_tn(kd16[h], vn16[h]) for h in heads]
        for h in heads:
            sl = slice(h * HEAD_W, (h + 1) * HEAD_W)
            rs = slice(c * chunk, (c + 1) * chunk)
            on = o[h] * lax.rsqrt(jnp.mean(o[h] * o[h], axis=-1, keepdims=True) + RMS_EPS) * ng_ref[...]
            o_ref[rs, sl] = (on * _silu(z_ref[rs, sl])).astype(BF16)
    for h in heads:
        s_ref[h] = s[h]

    @pl.when(n == n_chunks - 1)
    def _():
        sfin_ref[...] = s_ref[...]


def _delta_rule(dq, dk, dv, proj, beta, gc, s0, ng, seq_len, chunk, rows, col_z):
    t, d = dq.shape
    nb = t // seq_len
    n_steps = seq_len // rows
    n_heads = d // HEAD_W
    kern = functools.partial(_delta_rule_kernel, chunk=chunk, n_chunks=n_steps)
    tok = lambda b, n: (b * n_steps + n, 0)
    return pl.pallas_call(
        kern,
        grid=(nb, n_steps),
        in_specs=[
            pl.BlockSpec((rows, d), tok),
            pl.BlockSpec((rows, d), tok),
            pl.BlockSpec((rows, d), tok),
            pl.BlockSpec((rows, d), lambda b, n: (b * n_steps + n, col_z)),
            pl.BlockSpec((rows, LANES), tok),
            pl.BlockSpec((rows, LANES), tok),
            pl.BlockSpec((None, n_heads, HEAD_W, HEAD_W), lambda b, n: (b, 0, 0, 0)),
            pl.BlockSpec((1, HEAD_W), lambda b, n: (0, 0)),
        ],
        out_specs=[
            pl.BlockSpec((rows, d), tok),
            pl.BlockSpec((None, n_heads, HEAD_W, HEAD_W), lambda b, n: (b, 0, 0, 0)),
        ],
        out_shape=[jax.ShapeDtypeStruct((t, d), BF16),
                   jax.ShapeDtypeStruct((nb, n_heads, HEAD_W, HEAD_W), F32)],
        scratch_shapes=[pltpu.VMEM((n_heads, HEAD_W, HEAD_W), F32)],
        compiler_params=_cparams(("parallel", "arbitrary")),
        name="delta_rule",
    )(dq, dk, dv, proj, beta, gc, s0, ng)


def _lambda(lam_ref, lam_init):
    l = lam_ref[...]
    a = jnp.sum(l[0:1] * l[1:2], axis=-1, keepdims=True)
    b = jnp.sum(l[2:3] * l[3:4], axis=-1, keepdims=True)
    return jnp.exp(a) - jnp.exp(b) + lam_init


def _sub_norm(o, g, lam_init):
    return o * lax.rsqrt(jnp.mean(o * o, axis=-1, keepdims=True) + RMS_EPS) * g * (1.0 - lam_init)


def _attn_prompt_kernel(qt_ref, k_ref, vt_ref, lam_ref, ng_ref, o_ref, m_ref, acc_ref, sa_ref, sb_ref, *, lam_init):
    i = pl.program_id(2)
    bq = qt_ref.shape[1]
    bk = vt_ref.shape[-1]
    assert bq == 2 * bk
    qt = qt_ref[...]
    row = lax.broadcasted_iota(jnp.int32, qt.shape, 0)
    zero = jnp.zeros_like(qt)
    qm = (jnp.where(row < DIFF_QK, qt, zero), jnp.where(row >= DIFF_QK, qt, zero))
    m_ref[...] = jnp.full(m_ref.shape, NEG_INF, F32)
    acc_ref[...] = jnp.zeros(acc_ref.shape, F32)

    def scores(j, s_ref):
        kb = k_ref[pl.ds(pl.multiple_of(j * bk, bk), bk), :]
        for m in range(2):
            s_ref[m] = _dot(kb, qm[m])

    def absorb(j, s_ref, mask=None):
        vb = vt_ref[j]
        for m in range(2):
            s = s_ref[m]
            if mask is not None:
                s = jnp.where(mask, s, NEG_INF)
            m_prev = m_ref[m]
            m_new = jnp.maximum(m_prev, jnp.max(s, axis=0, keepdims=True))
            alpha = jnp.exp2(m_prev - m_new)
            p = jnp.exp2(s - m_new).astype(BF16)
            acc_ref[m] = alpha * acc_ref[m] + _dot(vb, p)
            m_ref[m] = m_new

    scores(0, sa_ref)

    def pair(j0):
        scores(j0 + 1, sb_ref)
        absorb(j0, sa_ref)
        scores(j0 + 2, sa_ref)
        absorb(j0 + 1, sb_ref)

    def body(g, carry):
        pair(4 * g)
        pair(4 * g + 2)
        return carry

    lax.fori_loop(0, i // 2, body, 0)

    @pl.when(i % 2 == 1)
    def _():
        pair(2 * i - 2)

    scores(2 * i + 1, sb_ref)
    ri = lax.broadcasted_iota(jnp.int32, (bk, bq), 0)
    ci = lax.broadcasted_iota(jnp.int32, (bk, bq), 1)
    absorb(2 * i, sa_ref, (ri // CHUNK) <= (ci // CHUNK))
    absorb(2 * i + 1, sb_ref, ((ri + bk) // CHUNK) <= (ci // CHUNK))

    lam = _lambda(lam_ref, lam_init)
    a0 = acc_ref[0]
    a1 = acc_ref[1]
    ot = a0[0:HEAD_W] / a0[HEAD_W:HEAD_W + 1] - lam * (a1[0:HEAD_W] / a1[HEAD_W:HEAD_W + 1])
    o_ref[...] = _sub_norm(ot.T, ng_ref[...], lam_init).astype(BF16)


def _attn_prompt(qt, ka, vt, lam4, ng, seq_len, bk, lam_init):
    d, t = qt.shape
    nb = t // seq_len
    bq = 2 * bk
    nq = seq_len // bq
    nk = seq_len // bk
    n_heads = d // HEAD_W
    kern = functools.partial(_attn_prompt_kernel, lam_init=lam_init)
    return pl.pallas_call(
        kern,
        grid=(nb, n_heads, nq),
        in_specs=[
            pl.BlockSpec((HEAD_W, bq), lambda b, h, i: (h, b * nq + i)),
            pl.BlockSpec((None, None, seq_len, HEAD_W), lambda b, h, i: (b, h, 0, 0)),
            pl.BlockSpec((None, None, nk, V_AUG, bk), lambda b, h, i: (b, h, 0, 0, 0)),
            pl.BlockSpec((4, DIFF_QK), lambda b, h, i: (0, 0)),
            pl.BlockSpec((1, HEAD_W), lambda b, h, i: (0, 0)),
        ],
        out_specs=pl.BlockSpec((bq, HEAD_W), lambda b, h, i: (b * nq + i, h)),
        out_shape=jax.ShapeDtypeStruct((t, d), BF16),
        scratch_shapes=[pltpu.VMEM((2, 1, bq), F32), pltpu.VMEM((2, V_AUG, bq), F32),
                        pltpu.VMEM((2, bk, bq), F32), pltpu.VMEM((2, bk, bq), F32)],
        compiler_params=_cparams(("parallel", "parallel", "arbitrary")),
        name="attn_prompt",
    )(qt, ka, vt, lam4, ng)


def _attn_cached_kernel(q_ref, kn_ref, vn_ref, ck_ref, cv_ref, lam_ref, ng_ref, o_ref, *, lam_init):
    q = q_ref[...]
    lane = lax.broadcasted_iota(jnp.int32, q.shape, 1)
    zero = jnp.zeros_like(q)
    kp = ck_ref[...].astype(BF16)
    vp = cv_ref[...].astype(BF16)
    kn = kn_ref[...]
    vn = vn_ref[...]
    outs = []
    for m in range(2):
        qm = jnp.where((lane < DIFF_QK) if m == 0 else (lane >= DIFF_QK), q, zero)
        sp = _dot_nt(qm, kp)
        sn = _dot_nt(qm, kn)
        mx = jnp.maximum(jnp.max(sp, axis=-1, keepdims=True), jnp.max(sn, axis=-1, keepdims=True))
        pp = jnp.exp(sp - mx)
        pn = jnp.exp(sn - mx)
        den = jnp.sum(pp, axis=-1, keepdims=True) + jnp.sum(pn, axis=-1, keepdims=True)
        outs.append((_dot(pp.astype(BF16), vp) + _dot(pn.astype(BF16), vn)) / den)
    lam = _lambda(lam_ref, lam_init)
    o = outs[0] - lam * outs[1]
    o_ref[...] = _sub_norm(o, ng_ref[...], lam_init).astype(BF16)


def _attn_cached(qa, ka, va, cache_k, cache_v, lam4, ng, seq_len, lam_init):
    t, d = qa.shape
    nb = t // seq_len
    past = cache_k.shape[1]
    n_heads = d // HEAD_W
    kern = functools.partial(_attn_cached_kernel, lam_init=lam_init)
    tok = lambda b, h: (b, h)
    return pl.pallas_call(
        kern,
        grid=(nb, n_heads),
        in_specs=[
            pl.BlockSpec((seq_len, HEAD_W), tok),
            pl.BlockSpec((seq_len, HEAD_W), tok),
            pl.BlockSpec((seq_len, HEAD_W), tok),
            pl.BlockSpec((None, past, HEAD_W), lambda b, h: (b, 0, h)),
            pl.BlockSpec((None, past, HEAD_W), lambda b, h: (b, 0, h)),
            pl.BlockSpec((4, DIFF_QK), lambda b, h: (0, 0)),
            pl.BlockSpec((1, HEAD_W), lambda b, h: (0, 0)),
        ],
        out_specs=pl.BlockSpec((seq_len, HEAD_W), tok),
        out_shape=jax.ShapeDtypeStruct((t, d), BF16),
        compiler_params=_cparams(("parallel", "parallel")),
        name="attn_cached",
    )(qa, ka, va, cache_k, cache_v, lam4, ng)


def _merge_out_kernel(x_ref, oa_ref, ob_ref, ga_ref, gb_ref, wo_ref, g2_ref, h_ref, xnt_ref):
    merged = (_sigmoid(ga_ref[...]) * oa_ref[...].astype(F32)
              + _sigmoid(gb_ref[...]) * ob_ref[...].astype(F32))
    h = x_ref[...] + _dot(merged.astype(BF16), wo_ref[...])
    h_ref[...] = h
    hn = h * lax.rsqrt(jnp.mean(h * h, axis=-1, keepdims=True) + RMS_EPS) * g2_ref[...]
    xnt_ref[...] = hn.T.astype(BF16)


def _merge_out(x, oa, ob, proj, w_out, g2, tm, col_ga, col_gb):
    t, d = x.shape
    row = lambda i: (i, 0)
    return pl.pallas_call(
        _merge_out_kernel,
        grid=(t // tm,),
        in_specs=[
            pl.BlockSpec((tm, d), row),
            pl.BlockSpec((tm, d), row),
            pl.BlockSpec((tm, d), row),
            pl.BlockSpec((tm, d), lambda i: (i, col_ga)),
            pl.BlockSpec((tm, d), lambda i: (i, col_gb)),
            pl.BlockSpec((d, d), lambda i: (0, 0)),
            pl.BlockSpec((1, d), lambda i: (0, 0)),
        ],
        out_specs=[pl.BlockSpec((tm, d), row), pl.BlockSpec((d, tm), lambda i: (0, i))],
        out_shape=[jax.ShapeDtypeStruct((t, d), F32), jax.ShapeDtypeStruct((d, t), BF16)],
        compiler_params=_cparams(("parallel",)),
        name="merge_out",
    )(x, oa, ob, proj, proj, w_out, g2)


def _top_values(parts, count):
    vals = []
    for r in range(count):
        m = parts[0]
        for p in parts[1:]:
            m = jnp.maximum(m, p)
        m = jnp.max(m, axis=0, keepdims=True)
        vals.append(m)
        if r + 1 < count:
            parts = [jnp.where(p == m, NEG_INF, p) for p in parts]
    return vals


def _sorting_network(n):
    pairs = []
    p = 1
    while p < n:
        k = p
        while k >= 1:
            for j in range(k % p, n - k, 2 * k):
                for i in range(min(k, n - j - k)):
                    if (i + j) // (2 * p) == (i + j + k) // (2 * p):
                        pairs.append((i + j, i + j + k))
            k //= 2
        p *= 2
    return pairs


def _top_values_sorted(x, count):
    m = x.shape[0] // SUBLANES
    assert count <= m
    tiles = [x[SUBLANES * v:SUBLANES * (v + 1)] for v in range(m)]
    for i, j in _sorting_network(m):
        tiles[i], tiles[j] = jnp.maximum(tiles[i], tiles[j]), jnp.minimum(tiles[i], tiles[j])
    sub = lax.broadcasted_iota(jnp.int32, tiles[0].shape, 0).astype(F32)
    vals = []
    for r in range(count):
        top = jnp.max(tiles[0], axis=0, keepdims=True)
        vals.append(top)
        live = count - r - 1
        if live > 0:
            first = jnp.min(jnp.where(tiles[0] == top, sub, float(SUBLANES)), axis=0, keepdims=True)
            pop = sub == first
            for v in range(live):
                tiles[v] = jnp.where(pop, tiles[v + 1], tiles[v])
    return vals


def _pair_candidates(sv0, sv1, count):
    k = sv0.shape[0]
    parts = []
    first_single = k
    for i in range(k):
        nj = min(k, count // (i + 1))
        if nj > 1:
            for j0 in range(0, nj, SUBLANES):
                parts.append(sv0[i:i + 1] + sv1[j0:j0 + SUBLANES])
        else:
            first_single = min(first_single, i)
    for i0 in range(first_single, k, SUBLANES):
        parts.append(sv0[i0:i0 + SUBLANES] + sv1[0:1])
    return parts


def _peer_route_kernel(xnt_ref, wqt_ref, sk_ref, s0_ref, s1_ref, tau_ref):
    pq = _dot(wqt_ref[...], xnt_ref[...]).astype(BF16)
    dh = sk_ref.shape[-1]
    taus = []
    for h in range(PEER_HEADS):
        sc = []
        for p in range(2):
            r0 = (h * 2 + p) * dh
            sc.append(_dot(sk_ref[h, p], pq[r0:r0 + dh]))
        sv0 = jnp.concatenate(_top_values_sorted(sc[0], PEER_TOPK), axis=0)
        sv1 = jnp.concatenate(_top_values_sorted(sc[1], PEER_TOPK), axis=0)
        tv = _top_values(_pair_candidates(sv0, sv1, PEER_TOPK + 1), PEER_TOPK + 1)
        mx = tv[0]
        z = jnp.exp(tv[0] - mx)
        for r in range(1, PEER_TOPK):
            z = z + jnp.exp(tv[r] - mx)
        logz = jnp.log(z)
        taus.append((0.5 * (tv[PEER_TOPK - 1] + tv[PEER_TOPK]) - mx - logz) * LOG2_E)
        s0_ref[h] = (sc[0] - sv0[0:1] - logz) * LOG2_E
        s1_ref[h] = (sc[1] - sv1[0:1]) * LOG2_E
    tau_ref[...] = jnp.concatenate(taus, axis=0)


def _peer_route(xnt, wqt, sk16, tm):
    d, t = xnt.shape
    qw = wqt.shape[0]
    tok = lambda i: (0, 0, i)
    return pl.pallas_call(
        _peer_route_kernel,
        grid=(t // tm,),
        in_specs=[
            pl.BlockSpec((d, tm), lambda i: (0, i)),
            pl.BlockSpec((qw, d), lambda i: (0, 0)),
            pl.BlockSpec(sk16.shape, lambda i: (0, 0, 0, 0)),
        ],
        out_specs=[
            pl.BlockSpec((PEER_HEADS, N_KEYS, tm), tok),
            pl.BlockSpec((PEER_HEADS, N_KEYS, tm), tok),
            pl.BlockSpec((PEER_HEADS, tm), lambda i: (0, i)),
        ],
        out_shape=[jax.ShapeDtypeStruct((PEER_HEADS, N_KEYS, t), F32),
                   jax.ShapeDtypeStruct((PEER_HEADS, N_KEYS, t), F32),
                   jax.ShapeDtypeStruct((PEER_HEADS, t), F32)],
        compiler_params=_cparams(("parallel",)),
        name="peer_route",
    )(xnt, wqt, sk16)


def _gelu(x):
    return 0.5 * x * (1.0 + lax.erf(x * (2.0 ** -0.5)))


PEER_EXPERT_BLOCK = 1024
GROUP_ROWS = 64


BF16_ROWS = 16


def _peer_mix_kernel(xnt_ref, u_ref, vt_ref, s0_ref, s1_ref, tau_ref, h_ref, y_ref, acc_ref, e1_ref, *, n_eblk):
    j = pl.program_id(1)
    na = s0_ref.shape[1]
    tm = xnt_ref.shape[1]

    @pl.when(j == 0)
    def _():
        acc_ref[...] = jnp.zeros(acc_ref.shape, F32)
        e1_ref[...] = jnp.exp2(s1_ref[...]).astype(BF16)

    xnt = xnt_ref[...]
    sub = 2 * N_KEYS
    n_sub = na * N_KEYS // sub
    lw = min(tm, LANES)
    grp = GROUP_ROWS // BF16_ROWS

    def activations(k):
        return _dot(u_ref[k * sub:(k + 1) * sub, :], xnt)

    def row16(x):
        return jnp.broadcast_to(x, (BF16_ROWS, lw)).astype(BF16)[None]

    act = activations(0)
    out = None
    for k in range(n_sub):
        act_next = activations(k + 1) if k + 1 < n_sub else None
        a_lo, a_hi = k * sub // N_KEYS, (k + 1) * sub // N_KEYS
        piece = {}
        for t0 in range(0, tm, lw):
            ts = slice(t0, t0 + lw)
            s0 = [s0_ref[h, a_lo:a_hi, ts] for h in range(PEER_HEADS)]
            e0 = [jnp.exp2(s0[h]) for h in range(PEER_HEADS)]
            e1_min = [jnp.exp2(tau_ref[h:h + 1, ts] - s0[h]) for h in range(PEER_HEADS)]
            for a in range(a_lo, a_hi):
                ar = slice(a - a_lo, a - a_lo + 1)
                rows = [(row16(e1_min[h][ar]), row16(e0[h][ar])) for h in range(PEER_HEADS)]
                for b0 in range(0, N_KEYS, GROUP_ROWS):
                    r0 = a * N_KEYS + b0 - k * sub
                    wgt = jnp.zeros((grp, BF16_ROWS, lw), BF16)
                    for h in range(PEER_HEADS):
                        e1 = e1_ref[h, b0:b0 + GROUP_ROWS, ts].reshape(grp, BF16_ROWS, lw)
                        wgt = wgt + jnp.where(e1 >= rows[h][0], e1, jnp.zeros_like(e1)) * rows[h][1]
                    gate = _gelu(act[r0:r0 + GROUP_ROWS, ts].astype(BF16))
                    piece[(r0, t0)] = wgt.reshape(GROUP_ROWS, lw) * gate
        gw = jnp.concatenate([jnp.concatenate([piece[(r0, t0)] for t0 in range(0, tm, lw)], axis=1)
                              for r0 in range(0, sub, GROUP_ROWS)], axis=0)
        part = _dot(vt_ref[:, k * sub:(k + 1) * sub], gw)
        out = part if out is None else out + part
        act = act_next
    acc_ref[...] += out

    @pl.when(j == n_eblk - 1)
    def _():
        y_ref[...] = h_ref[...] + acc_ref[...].T


def _peer_mix(xnt, u16, vt16, s0z, s1, tau, h, tm):
    d, t = xnt.shape
    n_exp = u16.shape[0]
    te = vt16.shape[2]
    na = te // N_KEYS
    n_eblk = n_exp // te
    kern = functools.partial(_peer_mix_kernel, n_eblk=n_eblk)
    return pl.pallas_call(
        kern,
        grid=(t // tm, n_eblk),
        in_specs=[
            pl.BlockSpec((d, tm), lambda i, j: (0, i)),
            pl.BlockSpec((te, d), lambda i, j: (j, 0)),
            pl.BlockSpec((None, d, te), lambda i, j: (j, 0, 0)),
            pl.BlockSpec((PEER_HEADS, na, tm), lambda i, j: (0, j, i)),
            pl.BlockSpec((PEER_HEADS, N_KEYS, tm), lambda i, j: (0, 0, i)),
            pl.BlockSpec((PEER_HEADS, tm), lambda i, j: (0, i)),
            pl.BlockSpec((tm, d), lambda i, j: (i, 0)),
        ],
        out_specs=pl.BlockSpec((tm, d), lambda i, j: (i, 0)),
        out_shape=jax.ShapeDtypeStruct((t, d), F32),
        scratch_shapes=[pltpu.VMEM((d, tm), F32), pltpu.VMEM((PEER_HEADS, N_KEYS, tm), BF16)],
        compiler_params=_cparams(("parallel", "arbitrary")),
        name="peer_mix",
    )(xnt, u16, vt16, s0z, s1, tau, h)


def _pick(n, pref):
    t = min(n, pref)
    while n % t:
        t //= 2
    return t


def _layer(x, conv_buf, s0, past_k, past_v, lam_init, wts):
    nb, seq_len, d = x.shape
    t = nb * seq_len
    n_heads = d // HEAD_W
    qkv_w = 3 * d
    x2 = x.reshape(t, d)

    tm = _pick(t, 2048)
    proj, proj_ba = _in_proj(x2, wts["norm_mix_g"], wts["w_main"], wts["w_ba"], tm, 1024)
    col ={name: idx for idx, name in enumerate(("q", "k", "v", "z", "fq", "fk", "fv", "ga", "gb"))}

    chunk = min(CHUNK, seq_len)
    tp = _pick(seq_len, 256)
    cbuf8 = jnp.concatenate(
        [jnp.zeros((nb, SUBLANES - (CONV_W - 1), qkv_w), F32), conv_buf.astype(F32)], axis=1)
    dq, dk, dv, beta, gc = _delta_prep(proj, proj_ba, cbuf8, wts["conv_w"], wts["alog_pad"], wts["dtb_pad"],
                                       seq_len, tp, chunk)
    prompt = past_k is None
    bk = _pick(seq_len // 2, 512)
    qa, kf, ka, va, vf = _attn_prep(proj, wts["gq2"], wts["gk2"], d, bk if prompt else _pick(t, 256),
                                col["fq"], col["fk"], col["fv"], seq_len, prompt)

    rows = min(seq_len, 4 * chunk)
    oa, s_new = _delta_rule(dq, dk, dv, proj, beta, gc, s0, wts["delta_norm_g"], seq_len, chunk, rows, col["z"])

    if prompt:
        ob = _attn_prompt(qa, ka, va, wts["lam4"], wts["diff_norm_g"], seq_len, bk, lam_init)
    else:
        ob = _attn_cached(qa, ka, va, past_k.reshape(nb, past_k.shape[1], d),
                          past_v.reshape(nb, past_v.shape[1], d), wts["lam4"], wts["diff_norm_g"],
                          seq_len, lam_init)

    tmo = _pick(t, 256)
    h, xnt = _merge_out(x2, oa, ob, proj, wts["w_out"], wts["norm_ffn_g"], tmo, col["ga"], col["gb"])
    s0z, s1, tau = _peer_route(xnt, wts["wqt"], wts["sk"], _pick(t, 256))
    y = _peer_mix(xnt, wts["u"], wts["vt"], s0z, s1, tau, h, _pick(t, 512))

    new_k = kf.reshape(nb, seq_len, n_heads, 2, DIFF_QK)
    new_v = vf.reshape(nb, seq_len, n_heads, HEAD_W)
    assert seq_len >= CONV_W - 1
    new_conv = proj.reshape(nb, seq_len, proj.shape[1])[:, seq_len - (CONV_W - 1):, :qkv_w]
    return y.reshape(nb, seq_len, d), new_k, new_v, s_new, new_conv


def _prep_weights(l, d, norm_mix_g, w_in, conv_w, a_log, dt_bias, delta_norm_g, q_norm_g, k_norm_g,
                  lq1, lk1, lq2, lk2, diff_norm_g, w_out, norm_ffn_g, peer_w_q, peer_sub_keys, peer_u, peer_v):
    n_heads = d // HEAD_W
    qkv_w = 3 * d
    w = w_in[l]
    o_z = qkv_w
    o_beta = o_z + d
    o_a = o_beta + n_heads
    o_fq = o_a + n_heads
    w_main = jnp.concatenate([w[:, :o_beta], w[:, o_fq:]], axis=1).astype(BF16)
    w_ba = jnp.pad(w[:, o_beta:o_fq], ((0, 0), (0, LANES - 2 * n_heads))).astype(BF16)

    def lane_pad(v):
        return jnp.pad(v[l].astype(F32), (n_heads, LANES - 2 * n_heads)).reshape(1, LANES)

    return {
        "norm_mix_g": norm_mix_g[l].reshape(1, d),
        "w_main": w_main,
        "w_ba": w_ba,
        "conv_w": conv_w[l],
        "alog_pad": lane_pad(a_log),
        "dtb_pad": lane_pad(dt_bias),
        "delta_norm_g": delta_norm_g[l].reshape(1, HEAD_W),
        "gq2": jnp.tile(q_norm_g[l], 2).reshape(1, HEAD_W),
        "gk2": jnp.tile(k_norm_g[l], 2).reshape(1, HEAD_W),
        "lam4": jnp.stack([lq1[l], lk1[l], lq2[l], lk2[l]]).astype(F32),
        "diff_norm_g": diff_norm_g[l].reshape(1, HEAD_W),
        "w_out": w_out[l].astype(BF16),
        "norm_ffn_g": norm_ffn_g[l].reshape(1, d),
        "wqt": peer_w_q[l].T.astype(BF16),
        "sk": peer_sub_keys[l].astype(BF16),
        "u": peer_u[l].astype(BF16),
        "vt": peer_v[l].reshape(-1, PEER_EXPERT_BLOCK, d).transpose(0, 2, 1).astype(BF16),
    }


def kernel(x_prompt, x_sample, cache_diff_k, cache_diff_v, state_delta_s, state_delta_conv, norm_mix_g, w_in, conv_w, delta_a_log, delta_dt_bias, delta_norm_g, diff_q_norm_g, diff_k_norm_g, diff_lambda_q1, diff_lambda_k1, diff_lambda_q2, diff_lambda_k2, diff_norm_g, w_out, norm_ffn_g, peer_w_q, peer_sub_keys, peer_u, peer_v):
    depth = w_in.shape[0]
    d = x_prompt.shape[-1]
    n_heads = d // HEAD_W
    yp, ys = x_prompt, x_sample
    outs_p, outs_s = [], []
    for l in range(depth):
        wts = _prep_weights(l, d, norm_mix_g, w_in, conv_w, delta_a_log, delta_dt_bias, delta_norm_g,
                            diff_q_norm_g, diff_k_norm_g, diff_lambda_q1, diff_lambda_k1, diff_lambda_q2,
                            diff_lambda_k2, diff_norm_g, w_out, norm_ffn_g, peer_w_q, peer_sub_keys,
                            peer_u, peer_v)
        lam_init = 0.8 - 0.6 * math.exp(-0.3 * l)
        conv0 = jnp.zeros((yp.shape[0], CONV_W - 1, 3 * d), F32)
        s0 = jnp.zeros((yp.shape[0], n_heads, HEAD_W, HEAD_W), F32)
        yp, *rest = _layer(yp, conv0, s0, None, None, lam_init, wts)
        outs_p.append(rest)
        ys, *rest = _layer(ys, state_delta_conv[l], state_delta_s[l], cache_diff_k[l], cache_diff_v[l],
                           lam_init, wts)
        outs_s.append(rest)
    stack = lambda outs, i: jnp.stack([o[i] for o in outs])
    return (yp, ys,
            stack(outs_p, 0), stack(outs_p, 1), stack(outs_p, 2), stack(outs_p, 3),
            stack(outs_s, 0), stack(outs_s, 1), stack(outs_s, 2), stack(outs_s, 3))
```

```python
import functools
import math

import jax
import jax.numpy as jnp
from jax import lax
from jax.experimental import pallas as pl
from jax.experimental.pallas import tpu as pltpu

F32 = jnp.float32
BF16 = jnp.bfloat16

RMS_EPS = 1e-6
NEG_INF = -1e30
LOG2_E = math.log2(math.e)
CHUNK = 64
HEAD_W = 128
N_HEADS = 8
DIFF_QK = 64
CONV_W = 4
PEER_HEADS = 8
N_KEYS = 128
PEER_TOPK = 16
LANES = 128
SUBLANES = 8
VMEM_LIMIT = 56 * 1024 * 1024


def _cparams(sem):
    return pltpu.CompilerParams(dimension_semantics=sem, vmem_limit_bytes=VMEM_LIMIT)


def _dot(a, b):
    return jnp.dot(a, b, preferred_element_type=F32)


def _dot_nt(a, b):
    return lax.dot_general(a, b, (((1,), (1,)), ((), ())), preferred_element_type=F32)


def _dot_tn(a, b):
    return lax.dot_general(a, b, (((0,), (0,)), ((), ())), preferred_element_type=F32)


def _sigmoid(x):
    return 1.0 / (1.0 + jnp.exp(-x))


def _silu(x):
    return x * _sigmoid(x)


def _in_proj_kernel(x_ref, g_ref, w_ref, wba_ref, o_ref, oba_ref, xn_ref):
    @pl.when(pl.program_id(1) == 0)
    def _():
        x = x_ref[...]
        r = lax.rsqrt(jnp.mean(x * x, axis=-1, keepdims=True) + RMS_EPS)
        xn = (x * r * g_ref[...]).astype(BF16)
        xn_ref[...] = xn
        oba_ref[...] = _dot(xn, wba_ref[...])

    o_ref[...] = _dot(xn_ref[...], w_ref[...])


def _in_proj(x, g, w_main, w_ba, tm, tn):
    t, d = x.shape
    nw = w_main.shape[1]
    return pl.pallas_call(
        _in_proj_kernel,
        grid=(t // tm, nw // tn),
        in_specs=[
            pl.BlockSpec((tm, d), lambda i, j: (i, 0)),
            pl.BlockSpec((1, d), lambda i, j: (0, 0)),
            pl.BlockSpec((d, tn), lambda i, j: (0, j)),
            pl.BlockSpec((d, LANES), lambda i, j: (0, 0)),
        ],
        out_specs=[
            pl.BlockSpec((tm, tn), lambda i, j: (i, j)),
            pl.BlockSpec((tm, LANES), lambda i, j: (i, 0)),
        ],
        out_shape=[jax.ShapeDtypeStruct((t, nw), F32), jax.ShapeDtypeStruct((t, LANES), F32)],
        scratch_shapes=[pltpu.VMEM((tm, d), BF16)],
        compiler_params=_cparams(("parallel", "arbitrary")),
        name="in_proj",
    )(x, g, w_main, w_ba)


def _delta_prep_kernel(cur_ref, prev_ref, cbuf_ref, cw_ref, ba_ref, alog_ref, dtb_ref,
                       q_ref, k_ref, v_ref, beta_ref, gc_ref, *, tiles_per_seq, chunk):
    tm = cur_ref.shape[0]
    qkv_w = cur_ref.shape[1]
    d = qkv_w // 3
    first = (pl.program_id(0) % tiles_per_seq) == 0
    col_w = 4 * HEAD_W
    for c0 in range(0, qkv_w, col_w):
        cur = cur_ref[:, c0:c0 + col_w]
        prev = jnp.where(first, cbuf_ref[:, c0:c0 + col_w], prev_ref[:, c0:c0 + col_w])
        ext = jnp.concatenate([prev, cur], axis=0)
        conv = cw_ref[0:1, c0:c0 + col_w] * ext[SUBLANES - 3:SUBLANES - 3 + tm]
        for j in range(1, CONV_W):
            conv = conv + cw_ref[j:j + 1, c0:c0 + col_w] * ext[SUBLANES - 3 + j:SUBLANES - 3 + j + tm]
        act = _silu(conv)
        for hh in range(col_w // HEAD_W):
            col = c0 + hh * HEAD_W
            a = act[:, hh * HEAD_W:(hh + 1) * HEAD_W]
            if col < 2 * d:
                a = a * lax.rsqrt(jnp.sum(a * a, axis=-1, keepdims=True) + RMS_EPS)
            if col < d:
                q_ref[:, col:col + HEAD_W] = (a * (HEAD_W ** -0.5)).astype(BF16)
            elif col < 2 * d:
                k_ref[:, col - d:col - d + HEAD_W] = a.astype(BF16)
            else:
                v_ref[:, col - 2 * d:col - 2 * d + HEAD_W] = a.astype(BF16)
    ba = ba_ref[...]
    beta_ref[...] = _sigmoid(ba)
    xs = ba + dtb_ref[...]
    softplus = jnp.maximum(xs, 0.0) + jnp.log(1.0 + jnp.exp(-jnp.abs(xs)))
    g = -jnp.exp(alog_ref[...]) * softplus
    row = lax.broadcasted_iota(jnp.int32, (tm, tm), 0)
    colm = lax.broadcasted_iota(jnp.int32, (tm, tm), 1)
    tri = jnp.where((row // chunk == colm // chunk) & (colm <= row), 1.0, 0.0).astype(F32)
    gc_ref[...] = jnp.dot(tri, g, preferred_element_type=F32, precision=lax.Precision.HIGHEST)


def _delta_prep(proj, proj_ba, cbuf8, conv_w, alog_pad, dtb_pad, seq_len, tm, chunk):
    t = proj.shape[0]
    qkv_w = conv_w.shape[1]
    d = qkv_w // 3
    tiles_per_seq = seq_len // tm
    rows8 = tm // SUBLANES
    kern = functools.partial(_delta_prep_kernel, tiles_per_seq=tiles_per_seq, chunk=chunk)
    return pl.pallas_call(
        kern,
        grid=(t // tm,),
        in_specs=[
            pl.BlockSpec((tm, qkv_w), lambda i: (i, 0)),
            pl.BlockSpec((SUBLANES, qkv_w), lambda i: (jnp.maximum(i * rows8 - 1, 0), 0)),
            pl.BlockSpec((None, SUBLANES, qkv_w), lambda i: (i // tiles_per_seq, 0, 0)),
            pl.BlockSpec((CONV_W, qkv_w), lambda i: (0, 0)),
            pl.BlockSpec((tm, LANES), lambda i: (i, 0)),
            pl.BlockSpec((1, LANES), lambda i: (0, 0)),
            pl.BlockSpec((1, LANES), lambda i: (0, 0)),
        ],
        out_specs=[
            pl.BlockSpec((tm, d), lambda i: (i, 0)),
            pl.BlockSpec((tm, d), lambda i: (i, 0)),
            pl.BlockSpec((tm, d), lambda i: (i, 0)),
            pl.BlockSpec((tm, LANES), lambda i: (i, 0)),
            pl.BlockSpec((tm, LANES), lambda i: (i, 0)),
        ],
        out_shape=[jax.ShapeDtypeStruct((t, d), BF16)] * 3 + [jax.ShapeDtypeStruct((t, LANES), F32)] * 2,
        compiler_params=_cparams(("parallel",)),
        name="delta_prep",
    )(proj, proj, cbuf8, conv_w, proj_ba, alog_pad, dtb_pad)


V_AUG = HEAD_W + 16


def _attn_prep_kernel(fq_ref, fk_ref, fv_ref, gq_ref, gk_ref, q_ref, kf_ref, ka_ref, v_ref, vf_ref, *, transposed):
    tm, d = fq_ref.shape
    vf_ref[...] = fv_ref[...]
    lane = lax.broadcasted_iota(jnp.int32, (tm, HEAD_W), 1)
    low = lane < DIFF_QK

    def qk_norm(x, g):
        s = x * x
        tot = jnp.sum(s, axis=-1, keepdims=True)
        lo = jnp.sum(jnp.where(low, s, 0.0), axis=-1, keepdims=True)
        ms = jnp.where(low, lo, tot - lo) * (1.0 / DIFF_QK)
        return x * lax.rsqrt(ms + RMS_EPS) * g

    for h in range(d // HEAD_W):
        sl = slice(h * HEAD_W, (h + 1) * HEAD_W)
        qn = qk_norm(fq_ref[:, sl], gq_ref[...]) * (DIFF_QK ** -0.5)
        kn = qk_norm(fk_ref[:, sl], gk_ref[...])
        kf_ref[:, sl] = kn
        if transposed:
            ka_ref[h] = kn.astype(BF16)
            q_ref[sl, :] = (qn * LOG2_E).T.astype(BF16)
            v_ref[h, 0:HEAD_W, :] = fv_ref[:, sl].T.astype(BF16)
            v_ref[h, HEAD_W:V_AUG, :] = jnp.ones((V_AUG - HEAD_W, tm), BF16)
        else:
            ka_ref[:, sl] = kn.astype(BF16)
            q_ref[:, sl] = qn.astype(BF16)
            v_ref[:, sl] = fv_ref[:, sl].astype(BF16)


def _attn_prep(proj, gq2, gk2, d, tm, col_q, col_k, col_v, seq_len, transposed):
    t = proj.shape[0]
    n_heads = d // HEAD_W
    row = pl.BlockSpec((tm, d), lambda i: (i, 0))
    if transposed:
        per_seq = seq_len // tm
        q_spec = pl.BlockSpec((d, tm), lambda i: (0, i))
        q_shape = jax.ShapeDtypeStruct((d, t), BF16)
        v_spec = pl.BlockSpec((None, n_heads, None, V_AUG, tm), lambda i: (i // per_seq, 0, i % per_seq, 0, 0))
        v_shape = jax.ShapeDtypeStruct((t // seq_len, n_heads, per_seq, V_AUG, tm), BF16)
        k_spec = pl.BlockSpec((None, n_heads, tm, HEAD_W), lambda i: (i // per_seq, 0, i % per_seq, 0))
        k_shape = jax.ShapeDtypeStruct((t // seq_len, n_heads, seq_len, HEAD_W), BF16)
    else:
        q_spec, q_shape = row, jax.ShapeDtypeStruct((t, d), BF16)
        v_spec, v_shape = row, jax.ShapeDtypeStruct((t, d), BF16)
        k_spec, k_shape = row, jax.ShapeDtypeStruct((t, d), BF16)
    return pl.pallas_call(
        functools.partial(_attn_prep_kernel, transposed=transposed),
        grid=(t // tm,),
        in_specs=[
            pl.BlockSpec((tm, d), lambda i: (i, col_q)),
            pl.BlockSpec((tm, d), lambda i: (i, col_k)),
            pl.BlockSpec((tm, d), lambda i: (i, col_v)),
            pl.BlockSpec((1, HEAD_W), lambda i: (0, 0)),
            pl.BlockSpec((1, HEAD_W), lambda i: (0, 0)),
        ],
        out_specs=[q_spec, row, k_spec, v_spec, row],
        out_shape=[q_shape, jax.ShapeDtypeStruct((t, d), F32), k_shape, v_shape,
                   jax.ShapeDtypeStruct((t, d), F32)],
        compiler_params=_cparams(("parallel",)),
        name="attn_prep",
    )(proj, proj, proj, gq2, gk2)


def _delta_rule_kernel(q_ref, k_ref, v_ref, z_ref, beta_ref, gc_ref, s0_ref, ng_ref,
                       o_ref, sfin_ref, s_ref, *, chunk, n_chunks):
    n = pl.program_id(1)
    rows = q_ref.shape[0]
    n_heads = q_ref.shape[1] // HEAD_W

    @pl.when(n == 0)
    def _():
        s_ref[...] = s0_ref[...]

    lane = lax.broadcasted_iota(jnp.int32, (rows, LANES), 1)
    ri = lax.broadcasted_iota(jnp.int32, (chunk, chunk), 0)
    ci = lax.broadcasted_iota(jnp.int32, (chunk, chunk), 1)
    eye = ri == ci
    incl = ci <= ri
    strict = ci < ri
    eye_f = jnp.where(eye, 1.0, 0.0).astype(F32)
    beta_all = beta_ref[...]
    gc_all = gc_ref[...]
    n_stage = int(math.log2(chunk))
    heads = range(n_heads)
    chunks = range(rows // chunk)
    items = [(h, c) for c in chunks for h in heads]

    beta_h = [jnp.sum(jnp.where(lane == h, beta_all, 0.0), axis=-1, keepdims=True) for h in heads]
    g_h = [jnp.sum(jnp.where(lane == n_heads + h, gc_all, 0.0), axis=-1, keepdims=True) for h in heads]
    q16, k16, kq, decay, eg, glast, gcol, rhs = {}, {}, {}, {}, {}, {}, {}, {}
    for it in items:
        h, c = it
        sl = slice(h * HEAD_W, (h + 1) * HEAD_W)
        rs = slice(c * chunk, (c + 1) * chunk)
        q16[it] = q_ref[rs, sl]
        k16[it] = k_ref[rs, sl]
        k = k16[it].astype(F32)
        beta = beta_h[h][rs]
        gcol[it] = g_h[h][rs]
        grow = jnp.sum(jnp.where(eye, gcol[it], 0.0), axis=0, keepdims=True)
        decay[it] = jnp.where(incl, jnp.exp(jnp.where(incl, gcol[it] - grow, 0.0)), 0.0)
        eg[it] = jnp.exp(gcol[it])
        glast[it] = gcol[it][chunk - 1:chunk]
        kb = k * beta
        kq[it] = jnp.concatenate([kb.astype(BF16), q16[it]], axis=0)
        rhs[it] = jnp.concatenate([(kb * eg[it]).astype(BF16),
                                   (v_ref[rs, sl].astype(F32) * beta).astype(BF16)], axis=1)
    kk = {it: _dot_nt(kq[it], k16[it]) for it in items}
    mp = {it: -jnp.where(strict, kk[it][0:chunk] * decay[it], 0.0) for it in items}
    tmat = {it: eye_f + mp[it] for it in items}
    for stage in range(1, n_stage):
        m16 = {it: mp[it].astype(BF16) for it in items}
        mp = {it: _dot(m16[it], m16[it]) for it in items}
        m16 = {it: mp[it].astype(BF16) for it in items}
        tmat = {it: tmat[it] + _dot(m16[it], tmat[it].astype(BF16)) for it in items}
    wu = {it: _dot(tmat[it].astype(BF16), rhs[it]) for it in items}
    intra16 = {it: (kk[it][chunk:2 * chunk] * decay[it]).astype(BF16) for it in items}

    s = [s_ref[h] for h in heads]
    for c in chunks:
        s16 = [s[h].astype(BF16) for h in heads]
        lhs = [jnp.concatenate([wu[(h, c)][:, 0:HEAD_W].astype(BF16),
                                (q16[(h, c)].astype(F32) * eg[(h, c)]).astype(BF16)], axis=0) for h in heads]
        ws = [_dot(lhs[h], s16[h]) for h in heads]
        vn16 = [(wu[(h, c)][:, HEAD_W:2 * HEAD_W] - ws[h][0:chunk]).astype(BF16) for h in heads]
        o = [ws[h][chunk:2 * chunk] + _dot(intra16[(h, c)], vn16[h]) for h in heads]
        kd16 = [(k16[(h, c)].astype(F32) * jnp.exp(glast[(h, c)] - gcol[(h, c)])).astype(BF16) for h in heads]
        s = [s[h] * jnp.exp(glast[(h, c)]) + _dot_tn(kd16[h], vn16[h]) for h in heads]
        for h in heads:
            sl = slice(h * HEAD_W, (h + 1) * HEAD_W)
            rs = slice(c * chunk, (c + 1) * chunk)
            on = o[h] * lax.rsqrt(jnp.mean(o[h] * o[h], axis=-1, keepdims=True) + RMS_EPS) * ng_ref[...]
            o_ref[rs, sl] = (on * _silu(z_ref[rs, sl])).astype(BF16)
    for h in heads:
        s_ref[h] = s[h]

    @pl.when(n == n_chunks - 1)
    def _():
        sfin_ref[...] = s_ref[...]


def _delta_rule(dq, dk, dv, proj, beta, gc, s0, ng, seq_len, chunk, rows, col_z):
    t, d = dq.shape
    nb = t // seq_len
    n_steps = seq_len // rows
    n_heads = d // HEAD_W
    kern = functools.partial(_delta_rule_kernel, chunk=chunk, n_chunks=n_steps)
    tok = lambda b, n: (b * n_steps + n, 0)
    return pl.pallas_call(
        kern,
        grid=(nb, n_steps),
        in_specs=[
            pl.BlockSpec((rows, d), tok),
            pl.BlockSpec((rows, d), tok),
            pl.BlockSpec((rows, d), tok),
            pl.BlockSpec((rows, d), lambda b, n: (b * n_steps + n, col_z)),
            pl.BlockSpec((rows, LANES), tok),
            pl.BlockSpec((rows, LANES), tok),
            pl.BlockSpec((None, n_heads, HEAD_W, HEAD_W), lambda b, n: (b, 0, 0, 0)),
            pl.BlockSpec((1, HEAD_W), lambda b, n: (0, 0)),
        ],
        out_specs=[
            pl.BlockSpec((rows, d), tok),
            pl.BlockSpec((None, n_heads, HEAD_W, HEAD_W), lambda b, n: (b, 0, 0, 0)),
        ],
        out_shape=[jax.ShapeDtypeStruct((t, d), BF16),
                   jax.ShapeDtypeStruct((nb, n_heads, HEAD_W, HEAD_W), F32)],
        scratch_shapes=[pltpu.VMEM((n_heads, HEAD_W, HEAD_W), F32)],
        compiler_params=_cparams(("parallel", "arbitrary")),
        name="delta_rule",
    )(dq, dk, dv, proj, beta, gc, s0, ng)


def _lambda(lam_ref, lam_init):
    l = lam_ref[...]
    a = jnp.sum(l[0:1] * l[1:2], axis=-1, keepdims=True)
    b = jnp.sum(l[2:3] * l[3:4], axis=-1, keepdims=True)
    return jnp.exp(a) - jnp.exp(b) + lam_init


def _sub_norm(o, g, lam_init):
    return o * lax.rsqrt(jnp.mean(o * o, axis=-1, keepdims=True) + RMS_EPS) * g * (1.0 - lam_init)


def _attn_prompt_kernel(qt_ref, k_ref, vt_ref, lam_ref, ng_ref, o_ref, m_ref, acc_ref, sa_ref, sb_ref, *, lam_init):
    i = pl.program_id(2)
    bq = qt_ref.shape[1]
    bk = vt_ref.shape[-1]
    assert bq == 2 * bk
    qt = qt_ref[...]
    row = lax.broadcasted_iota(jnp.int32, qt.shape, 0)
    zero = jnp.zeros_like(qt)
    qm = (jnp.where(row < DIFF_QK, qt, zero), jnp.where(row >= DIFF_QK, qt, zero))
    m_ref[...] = jnp.full(m_ref.shape, NEG_INF, F32)
    acc_ref[...] = jnp.zeros(acc_ref.shape, F32)

    full = slice(0, bq)
    half = slice(bk, bq)
    qm_half = tuple(x[:, half] for x in qm)

    def scores(j, s_ref, qs=full):
        kb = k_ref[pl.ds(pl.multiple_of(j * bk, bk), bk), :]
        for m in range(2):
            s_ref[m, :, qs] = _dot(kb, qm[m] if qs is full else qm_half[m])

    def absorb(j, s_ref, mask=None, qs=full):
        vb = vt_ref[j]
        for m in range(2):
            s = s_ref[m, :, qs]
            if mask is not None:
                s = jnp.where(mask, s, NEG_INF)
            m_prev = m_ref[m, :, qs]
            m_new = jnp.maximum(m_prev, jnp.max(s, axis=0, keepdims=True))
            alpha = jnp.exp2(m_prev - m_new)
            p = jnp.exp2(s - m_new[0:1]).astype(BF16)
            acc_ref[m, :, qs] = alpha[0:1] * acc_ref[m, :, qs] + _dot(vb, p)
            m_ref[m, :, qs] = m_new

    scores(0, sa_ref)

    def pair(j0):
        scores(j0 + 1, sb_ref)
        absorb(j0, sa_ref)
        scores(j0 + 2, sa_ref)
        absorb(j0 + 1, sb_ref)

    def body(g, carry):
        pair(4 * g)
        pair(4 * g + 2)
        return carry

    lax.fori_loop(0, i // 2, body, 0)

    @pl.when(i % 2 == 1)
    def _():
        pair(2 * i - 2)

    scores(2 * i + 1, sb_ref, half)
    ri = lax.broadcasted_iota(jnp.int32, (bk, bq), 0)
    ci = lax.broadcasted_iota(jnp.int32, (bk, bq), 1)
    absorb(2 * i, sa_ref, (ri // CHUNK) <= (ci // CHUNK))
    rh = lax.broadcasted_iota(jnp.int32, (bk, bk), 0)
    ch = lax.broadcasted_iota(jnp.int32, (bk, bk), 1)
    absorb(2 * i + 1, sb_ref, (rh // CHUNK) <= (ch // CHUNK), half)

    lam = _lambda(lam_ref, lam_init)
    a0 = acc_ref[0]
    a1 = acc_ref[1]
    ot = a0[0:HEAD_W] / a0[HEAD_W:HEAD_W + 1] - lam * (a1[0:HEAD_W] / a1[HEAD_W:HEAD_W + 1])
    o_ref[...] = _sub_norm(ot.T, ng_ref[...], lam_init).astype(BF16)


def _attn_prompt(qt, ka, vt, lam4, ng, seq_len, bk, lam_init):
    d, t = qt.shape
    nb = t // seq_len
    bq = 2 * bk
    nq = seq_len // bq
    nk = seq_len // bk
    n_heads = d // HEAD_W
    kern = functools.partial(_attn_prompt_kernel, lam_init=lam_init)
    return pl.pallas_call(
        kern,
        grid=(nb, n_heads, nq),
        in_specs=[
            pl.BlockSpec((HEAD_W, bq), lambda b, h, i: (h, b * nq + i)),
            pl.BlockSpec((None, None, seq_len, HEAD_W), lambda b, h, i: (b, h, 0, 0)),
            pl.BlockSpec((None, None, nk, V_AUG, bk), lambda b, h, i: (b, h, 0, 0, 0)),
            pl.BlockSpec((4, DIFF_QK), lambda b, h, i: (0, 0)),
            pl.BlockSpec((1, HEAD_W), lambda b, h, i: (0, 0)),
        ],
        out_specs=pl.BlockSpec((bq, HEAD_W), lambda b, h, i: (b * nq + i, h)),
        out_shape=jax.ShapeDtypeStruct((t, d), BF16),
        scratch_shapes=[pltpu.VMEM((2, SUBLANES, bq), F32), pltpu.VMEM((2, V_AUG, bq), F32),
                        pltpu.VMEM((2, bk, bq), F32), pltpu.VMEM((2, bk, bq), F32)],
        compiler_params=_cparams(("parallel", "parallel", "arbitrary")),
        name="attn_prompt",
    )(qt, ka, vt, lam4, ng)


def _attn_cached_kernel(q_ref, kn_ref, vn_ref, ck_ref, cv_ref, lam_ref, ng_ref, o_ref, *, lam_init):
    rows = q_ref.shape[0]
    lane = lax.broadcasted_iota(jnp.int32, (rows, HEAD_W), 1)
    lam = _lambda(lam_ref, lam_init)
    for h in range(q_ref.shape[1] // HEAD_W):
        sl = slice(h * HEAD_W, (h + 1) * HEAD_W)
        q = q_ref[:, sl]
        zero = jnp.zeros_like(q)
        kp = ck_ref[:, sl].astype(BF16)
        vp = cv_ref[:, sl].astype(BF16)
        kn = kn_ref[:, sl]
        vn = vn_ref[:, sl]
        outs = []
        for m in range(2):
            qm = jnp.where((lane < DIFF_QK) if m == 0 else (lane >= DIFF_QK), q, zero)
            sp = _dot_nt(qm, kp)
            sn = _dot_nt(qm, kn)
            mx = jnp.maximum(jnp.max(sp, axis=-1, keepdims=True), jnp.max(sn, axis=-1, keepdims=True))
            pp = jnp.exp(sp - mx)
            pn = jnp.exp(sn - mx)
            den = jnp.sum(pp, axis=-1, keepdims=True) + jnp.sum(pn, axis=-1, keepdims=True)
            outs.append((_dot(pp.astype(BF16), vp) + _dot(pn.astype(BF16), vn)) / den)
        o = outs[0] - lam * outs[1]
        o_ref[:, sl] = _sub_norm(o, ng_ref[...], lam_init).astype(BF16)


def _attn_cached(qa, ka, va, cache_k, cache_v, lam4, ng, seq_len, lam_init):
    t, d = qa.shape
    nb = t // seq_len
    past = cache_k.shape[1]
    kern = functools.partial(_attn_cached_kernel, lam_init=lam_init)
    tok = pl.BlockSpec((seq_len, d), lambda b: (b, 0))
    return pl.pallas_call(
        kern,
        grid=(nb,),
        in_specs=[
            tok, tok, tok,
            pl.BlockSpec((None, past, d), lambda b: (b, 0, 0)),
            pl.BlockSpec((None, past, d), lambda b: (b, 0, 0)),
            pl.BlockSpec((4, DIFF_QK), lambda b: (0, 0)),
            pl.BlockSpec((1, HEAD_W), lambda b: (0, 0)),
        ],
        out_specs=tok,
        out_shape=jax.ShapeDtypeStruct((t, d), BF16),
        compiler_params=_cparams(("parallel",)),
        name="attn_cached",
    )(qa, ka, va, cache_k, cache_v, lam4, ng)


def _merge_out_kernel(x_ref, oa_ref, ob_ref, ga_ref, gb_ref, wo_ref, g2_ref, h_ref, xnt_ref):
    merged = (_sigmoid(ga_ref[...]) * oa_ref[...].astype(F32)
              + _sigmoid(gb_ref[...]) * ob_ref[...].astype(F32))
    h = x_ref[...] + _dot(merged.astype(BF16), wo_ref[...])
    h_ref[...] = h
    hn = h * lax.rsqrt(jnp.mean(h * h, axis=-1, keepdims=True) + RMS_EPS) * g2_ref[...]
    xnt_ref[...] = hn.T.astype(BF16)


def _merge_out(x, oa, ob, proj, w_out, g2, tm, col_ga, col_gb):
    t, d = x.shape
    row = lambda i: (i, 0)
    return pl.pallas_call(
        _merge_out_kernel,
        grid=(t // tm,),
        in_specs=[
            pl.BlockSpec((tm, d), row),
            pl.BlockSpec((tm, d), row),
            pl.BlockSpec((tm, d), row),
            pl.BlockSpec((tm, d), lambda i: (i, col_ga)),
            pl.BlockSpec((tm, d), lambda i: (i, col_gb)),
            pl.BlockSpec((d, d), lambda i: (0, 0)),
            pl.BlockSpec((1, d), lambda i: (0, 0)),
        ],
        out_specs=[pl.BlockSpec((tm, d), row), pl.BlockSpec((d, tm), lambda i: (0, i))],
        out_shape=[jax.ShapeDtypeStruct((t, d), F32), jax.ShapeDtypeStruct((d, t), BF16)],
        compiler_params=_cparams(("parallel",)),
        name="merge_out",
    )(x, oa, ob, proj, proj, w_out, g2)


def _top_values(parts, count):
    vals = []
    for r in range(count):
        m = parts[0]
        for p in parts[1:]:
            m = jnp.maximum(m, p)
        m = jnp.max(m, axis=0, keepdims=True)
        vals.append(m)
        if r + 1 < count:
            parts = [jnp.where(p == m, NEG_INF, p) for p in parts]
    return vals


def _sorting_network(n):
    pairs = []
    p = 1
    while p < n:
        k = p
        while k >= 1:
            for j in range(k % p, n - k, 2 * k):
                for i in range(min(k, n - j - k)):
                    if (i + j) // (2 * p) == (i + j + k) // (2 * p):
                        pairs.append((i + j, i + j + k))
            k //= 2
        p *= 2
    return pairs


def _top_values_sorted(x, count):
    m = x.shape[0] // SUBLANES
    assert count <= m
    tiles = [x[SUBLANES * v:SUBLANES * (v + 1)] for v in range(m)]
    for i, j in _sorting_network(m):
        tiles[i], tiles[j] = jnp.maximum(tiles[i], tiles[j]), jnp.minimum(tiles[i], tiles[j])
    sub = lax.broadcasted_iota(jnp.int32, tiles[0].shape, 0).astype(F32)
    vals = []
    for r in range(count):
        top = jnp.max(tiles[0], axis=0, keepdims=True)
        vals.append(top)
        live = count - r - 1
        if live > 0:
            first = jnp.min(jnp.where(tiles[0] == top, sub, float(SUBLANES)), axis=0, keepdims=True)
            pop = sub == first
            for v in range(live):
                tiles[v] = jnp.where(pop, tiles[v + 1], tiles[v])
    return vals


def _pair_candidates(sv0, sv1, count):
    k = sv0.shape[0]
    parts = []
    first_single = k
    for i in range(k):
        nj = min(k, count // (i + 1))
        if nj > 1:
            for j0 in range(0, nj, SUBLANES):
                parts.append(sv0[i:i + 1] + sv1[j0:j0 + SUBLANES])
        else:
            first_single = min(first_single, i)
    for i0 in range(first_single, k, SUBLANES):
        parts.append(sv0[i0:i0 + SUBLANES] + sv1[0:1])
    return parts


def _peer_route_kernel(xnt_ref, wqt_ref, sk_ref, s0_ref, s1_ref, tau_ref):
    pq = _dot(wqt_ref[...], xnt_ref[...]).astype(BF16)
    dh = sk_ref.shape[-1]
    taus = []
    for h in range(PEER_HEADS):
        sc = []
        for p in range(2):
            r0 = (h * 2 + p) * dh
            sc.append(_dot(sk_ref[h, p], pq[r0:r0 + dh]))
        sv0 = jnp.concatenate(_top_values_sorted(sc[0], PEER_TOPK), axis=0)
        sv1 = jnp.concatenate(_top_values_sorted(sc[1], PEER_TOPK), axis=0)
        tv = _top_values(_pair_candidates(sv0, sv1, PEER_TOPK + 1), PEER_TOPK + 1)
        mx = tv[0]
        z = jnp.exp(tv[0] - mx)
        for r in range(1, PEER_TOPK):
            z = z + jnp.exp(tv[r] - mx)
        logz = jnp.log(z)
        taus.append((0.5 * (tv[PEER_TOPK - 1] + tv[PEER_TOPK]) - mx - logz) * LOG2_E)
        s0_ref[h] = (sc[0] - sv0[0:1] - logz) * LOG2_E
        s1_ref[h] = (sc[1] - sv1[0:1]) * LOG2_E
    tau_ref[...] = jnp.concatenate(taus, axis=0)


def _peer_route(xnt, wqt, sk16, tm):
    d, t = xnt.shape
    qw = wqt.shape[0]
    tok = lambda i: (0, 0, i)
    return pl.pallas_call(
        _peer_route_kernel,
        grid=(t // tm,),
        in_specs=[
            pl.BlockSpec((d, tm), lambda i: (0, i)),
            pl.BlockSpec((qw, d), lambda i: (0, 0)),
            pl.BlockSpec(sk16.shape, lambda i: (0, 0, 0, 0)),
        ],
        out_specs=[
            pl.BlockSpec((PEER_HEADS, N_KEYS, tm), tok),
            pl.BlockSpec((PEER_HEADS, N_KEYS, tm), tok),
            pl.BlockSpec((PEER_HEADS, tm), lambda i: (0, i)),
        ],
        out_shape=[jax.ShapeDtypeStruct((PEER_HEADS, N_KEYS, t), F32),
                   jax.ShapeDtypeStruct((PEER_HEADS, N_KEYS, t), F32),
                   jax.ShapeDtypeStruct((PEER_HEADS, t), F32)],
        compiler_params=_cparams(("parallel",)),
        name="peer_route",
    )(xnt, wqt, sk16)


def _gelu(x):
    return 0.5 * x * (1.0 + lax.erf(x * (2.0 ** -0.5)))


PEER_EXPERT_BLOCK = 2048
GROUP_ROWS = 64


BF16_ROWS = 16


def _peer_mix_kernel(xnt_ref, u_ref, vt_ref, s0_ref, s1_ref, tau_ref, h_ref, y_ref, acc_ref, e1_ref, *, n_eblk):
    j = pl.program_id(1)
    na = s0_ref.shape[1]
    tm = xnt_ref.shape[1]

    @pl.when(j == 0)
    def _():
        acc_ref[...] = jnp.zeros(acc_ref.shape, F32)
        e1_ref[...] = jnp.exp2(s1_ref[...]).astype(BF16)

    xnt = xnt_ref[...]
    sub = 2 * N_KEYS
    n_sub = na * N_KEYS // sub
    lw = min(tm, LANES)
    grp = GROUP_ROWS // BF16_ROWS

    def activations(k):
        return _dot(u_ref[k * sub:(k + 1) * sub, :], xnt)

    def row16(x):
        return jnp.broadcast_to(x, (BF16_ROWS, lw)).astype(BF16)[None]

    act = activations(0)
    out = None
    for k in range(n_sub):
        act_next = activations(k + 1) if k + 1 < n_sub else None
        a_lo, a_hi = k * sub // N_KEYS, (k + 1) * sub // N_KEYS
        piece = {}
        for t0 in range(0, tm, lw):
            ts = slice(t0, t0 + lw)
            s0 = [s0_ref[h, a_lo:a_hi, ts] for h in range(PEER_HEADS)]
            e0 = [jnp.exp2(s0[h]) for h in range(PEER_HEADS)]
            e1_min = [jnp.exp2(tau_ref[h:h + 1, ts] - s0[h]) for h in range(PEER_HEADS)]
            for a in range(a_lo, a_hi):
                ar = slice(a - a_lo, a - a_lo + 1)
                rows = [(row16(e1_min[h][ar]), row16(e0[h][ar])) for h in range(PEER_HEADS)]
                for b0 in range(0, N_KEYS, GROUP_ROWS):
                    r0 = a * N_KEYS + b0 - k * sub
                    wgt = jnp.zeros((grp, BF16_ROWS, lw), BF16)
                    for h in range(PEER_HEADS):
                        e1 = e1_ref[h, b0:b0 + GROUP_ROWS, ts].reshape(grp, BF16_ROWS, lw)
                        wgt = wgt + jnp.where(e1 >= rows[h][0], e1, jnp.zeros_like(e1)) * rows[h][1]
                    gate = _gelu(act[r0:r0 + GROUP_ROWS, ts].astype(BF16))
                    piece[(r0, t0)] = wgt.reshape(GROUP_ROWS, lw) * gate
        gw = jnp.concatenate([jnp.concatenate([piece[(r0, t0)] for t0 in range(0, tm, lw)], axis=1)
                              for r0 in range(0, sub, GROUP_ROWS)], axis=0)
        part = _dot(vt_ref[:, k * sub:(k + 1) * sub], gw)
        out = part if out is None else out + part
        act = act_next
    acc_ref[...] += out

    @pl.when(j == n_eblk - 1)
    def _():
        y_ref[...] = h_ref[...] + acc_ref[...].T


def _peer_mix(xnt, u16, vt16, s0z, s1, tau, h, tm):
    d, t = xnt.shape
    n_exp = u16.shape[0]
    te = vt16.shape[2]
    na = te // N_KEYS
    n_eblk = n_exp // te
    kern = functools.partial(_peer_mix_kernel, n_eblk=n_eblk)
    return pl.pallas_call(
        kern,
        grid=(t // tm, n_eblk),
        in_specs=[
            pl.BlockSpec((d, tm), lambda i, j: (0, i)),
            pl.BlockSpec((te, d), lambda i, j: (j, 0)),
            pl.BlockSpec((None, d, te), lambda i, j: (j, 0, 0)),
            pl.BlockSpec((PEER_HEADS, na, tm), lambda i, j: (0, j, i)),
            pl.BlockSpec((PEER_HEADS, N_KEYS, tm), lambda i, j: (0, 0, i)),
            pl.BlockSpec((PEER_HEADS, tm), lambda i, j: (0, i)),
            pl.BlockSpec((tm, d), lambda i, j: (i, 0)),
        ],
        out_specs=pl.BlockSpec((tm, d), lambda i, j: (i, 0)),
        out_shape=jax.ShapeDtypeStruct((t, d), F32),
        scratch_shapes=[pltpu.VMEM((d, tm), F32), pltpu.VMEM((PEER_HEADS, N_KEYS, tm), BF16)],
        compiler_params=_cparams(("parallel", "arbitrary")),
        name="peer_mix",
    )(xnt, u16, vt16, s0z, s1, tau, h)


def _pick(n, pref):
    t = min(n, pref)
    while n % t:
        t //= 2
    return t


def _layer(x, conv_buf, s0, past_k, past_v, lam_init, wts):
    nb, seq_len, d = x.shape
    t = nb * seq_len
    n_heads = d // HEAD_W
    qkv_w = 3 * d
    x2 = x.reshape(t, d)

    tm = _pick(t, 2048)
    proj, proj_ba = _in_proj(x2, wts["norm_mix_g"], wts["w_main"], wts["w_ba"], tm, 1024)
    col ={name: idx for idx, name in enumerate(("q", "k", "v", "z", "fq", "fk", "fv", "ga", "gb"))}

    chunk = min(CHUNK, seq_len)
    tp = _pick(seq_len, 256)
    cbuf8 = jnp.concatenate(
        [jnp.zeros((nb, SUBLANES - (CONV_W - 1), qkv_w), F32), conv_buf.astype(F32)], axis=1)
    dq, dk, dv, beta, gc = _delta_prep(proj, proj_ba, cbuf8, wts["conv_w"], wts["alog_pad"], wts["dtb_pad"],
                                       seq_len, tp, chunk)
    prompt = past_k is None
    bk = _pick(seq_len // 2, 512)
    qa, kf, ka, va, vf = _attn_prep(proj, wts["gq2"], wts["gk2"], d, bk if prompt else _pick(t, 256),
                                col["fq"], col["fk"], col["fv"], seq_len, prompt)

    rows = min(seq_len, 4 * chunk)
    oa, s_new = _delta_rule(dq, dk, dv, proj, beta, gc, s0, wts["delta_norm_g"], seq_len, chunk, rows, col["z"])

    if prompt:
        ob = _attn_prompt(qa, ka, va, wts["lam4"], wts["diff_norm_g"], seq_len, bk, lam_init)
    else:
        ob = _attn_cached(qa, ka, va, past_k.reshape(nb, past_k.shape[1], d),
                          past_v.reshape(nb, past_v.shape[1], d), wts["lam4"], wts["diff_norm_g"],
                          seq_len, lam_init)

    tmo = _pick(t, 256)
    h, xnt = _merge_out(x2, oa, ob, proj, wts["w_out"], wts["norm_ffn_g"], tmo, col["ga"], col["gb"])
    s0z, s1, tau = _peer_route(xnt, wts["wqt"], wts["sk"], _pick(t, 256))
    y = _peer_mix(xnt, wts["u"], wts["vt"], s0z, s1, tau, h, _pick(t, 512))

    new_k = kf.reshape(nb, seq_len, n_heads, 2, DIFF_QK)
    new_v = vf.reshape(nb, seq_len, n_heads, HEAD_W)
    assert seq_len >= CONV_W - 1
    new_conv = proj.reshape(nb, seq_len, proj.shape[1])[:, seq_len - (CONV_W - 1):, :qkv_w]
    return y.reshape(nb, seq_len, d), new_k, new_v, s_new, new_conv


def _prep_weights(l, d, norm_mix_g, w_in, conv_w, a_log, dt_bias, delta_norm_g, q_norm_g, k_norm_g,
                  lq1, lk1, lq2, lk2, diff_norm_g, w_out, norm_ffn_g, peer_w_q, peer_sub_keys, peer_u, peer_v):
    n_heads = d // HEAD_W
    qkv_w = 3 * d
    w = w_in[l]
    o_z = qkv_w
    o_beta = o_z + d
    o_a = o_beta + n_heads
    o_fq = o_a + n_heads
    w_main = jnp.concatenate([w[:, :o_beta], w[:, o_fq:]], axis=1).astype(BF16)
    w_ba = jnp.pad(w[:, o_beta:o_fq], ((0, 0), (0, LANES - 2 * n_heads))).astype(BF16)

    def lane_pad(v):
        return jnp.pad(v[l].astype(F32), (n_heads, LANES - 2 * n_heads)).reshape(1, LANES)

    return {
        "norm_mix_g": norm_mix_g[l].reshape(1, d),
        "w_main": w_main,
        "w_ba": w_ba,
        "conv_w": conv_w[l],
        "alog_pad": lane_pad(a_log),
        "dtb_pad": lane_pad(dt_bias),
        "delta_norm_g": delta_norm_g[l].reshape(1, HEAD_W),
        "gq2": jnp.tile(q_norm_g[l], 2).reshape(1, HEAD_W),
        "gk2": jnp.tile(k_norm_g[l], 2).reshape(1, HEAD_W),
        "lam4": jnp.stack([lq1[l], lk1[l], lq2[l], lk2[l]]).astype(F32),
        "diff_norm_g": diff_norm_g[l].reshape(1, HEAD_W),
        "w_out": w_out[l].astype(BF16),
        "norm_ffn_g": norm_ffn_g[l].reshape(1, d),
        "wqt": peer_w_q[l].T.astype(BF16),
        "sk": peer_sub_keys[l].astype(BF16),
        "u": peer_u[l].astype(BF16),
        "vt": peer_v[l].reshape(-1, PEER_EXPERT_BLOCK, d).transpose(0, 2, 1).astype(BF16),
    }


def kernel(x_prompt, x_sample, cache_diff_k, cache_diff_v, state_delta_s, state_delta_conv, norm_mix_g, w_in, conv_w, delta_a_log, delta_dt_bias, delta_norm_g, diff_q_norm_g, diff_k_norm_g, diff_lambda_q1, diff_lambda_k1, diff_lambda_q2, diff_lambda_k2, diff_norm_g, w_out, norm_ffn_g, peer_w_q, peer_sub_keys, peer_u, peer_v):
    depth = w_in.shape[0]
    d = x_prompt.shape[-1]
    n_heads = d // HEAD_W
    yp, ys = x_prompt, x_sample
    outs_p, outs_s = [], []
    for l in range(depth):
        wts = _prep_weights(l, d, norm_mix_g, w_in, conv_w, delta_a_log, delta_dt_bias, delta_norm_g,
                            diff_q_norm_g, diff_k_norm_g, diff_lambda_q1, diff_lambda_k1, diff_lambda_q2,
                            diff_lambda_k2, diff_norm_g, w_out, norm_ffn_g, peer_w_q, peer_sub_keys,
                            peer_u, peer_v)
        lam_init = 0.8 - 0.6 * math.exp(-0.3 * l)
        conv0 = jnp.zeros((yp.shape[0], CONV_W - 1, 3 * d), F32)
        s0 = jnp.zeros((yp.shape[0], n_heads, HEAD_W, HEAD_W), F32)
        yp, *rest = _layer(yp, conv0, s0, None, None, lam_init, wts)
        outs_p.append(rest)
        ys, *rest = _layer(ys, state_delta_conv[l], state_delta_s[l], cache_diff_k[l], cache_diff_v[l],
                           lam_init, wts)
        outs_s.append(rest)
    stack = lambda outs, i: jnp.stack([o[i] for o in outs])
    return (yp, ys,
            stack(outs_p, 0), stack(outs_p, 1), stack(outs_p, 2), stack(outs_p, 3),
            stack(outs_s, 0), stack(outs_s, 1), stack(outs_s, 2), stack(outs_s, 3))
```

```python
import functools
import math

import jax
import jax.numpy as jnp
from jax import lax
from jax.experimental import pallas as pl
from jax.experimental.pallas import tpu as pltpu

F32 = jnp.float32
BF16 = jnp.bfloat16

RMS_EPS = 1e-6
NEG_INF = -1e30
LOG2_E = math.log2(math.e)
CHUNK = 64
HEAD_W = 128
N_HEADS = 8
DIFF_QK = 64
CONV_W = 4
PEER_HEADS = 8
N_KEYS = 128
PEER_TOPK = 16
LANES = 128
SUBLANES = 8
VMEM_LIMIT = 56 * 1024 * 1024


def _cparams(sem):
    return pltpu.CompilerParams(dimension_semantics=sem, vmem_limit_bytes=VMEM_LIMIT)


def _dot(a, b):
    return jnp.dot(a, b, preferred_element_type=F32)


def _dot_nt(a, b):
    return lax.dot_general(a, b, (((1,), (1,)), ((), ())), preferred_element_type=F32)


def _dot_tn(a, b):
    return lax.dot_general(a, b, (((0,), (0,)), ((), ())), preferred_element_type=F32)


def _sigmoid(x):
    return 1.0 / (1.0 + jnp.exp(-x))


def _silu(x):
    return x * _sigmoid(x)


def _in_proj_kernel(x_ref, g_ref, w_ref, wba_ref, o_ref, oba_ref, xn_ref):
    @pl.when(pl.program_id(1) == 0)
    def _():
        x = x_ref[...]
        r = lax.rsqrt(jnp.mean(x * x, axis=-1, keepdims=True) + RMS_EPS)
        xn = (x * r * g_ref[...]).astype(BF16)
        xn_ref[...] = xn
        oba_ref[...] = _dot(xn, wba_ref[...])

    o_ref[...] = _dot(xn_ref[...], w_ref[...])


def _in_proj(x, g, w_main, w_ba, tm, tn):
    t, d = x.shape
    nw = w_main.shape[1]
    return pl.pallas_call(
        _in_proj_kernel,
        grid=(t // tm, nw // tn),
        in_specs=[
            pl.BlockSpec((tm, d), lambda i, j: (i, 0)),
            pl.BlockSpec((1, d), lambda i, j: (0, 0)),
            pl.BlockSpec((d, tn), lambda i, j: (0, j)),
            pl.BlockSpec((d, LANES), lambda i, j: (0, 0)),
        ],
        out_specs=[
            pl.BlockSpec((tm, tn), lambda i, j: (i, j)),
            pl.BlockSpec((tm, LANES), lambda i, j: (i, 0)),
        ],
        out_shape=[jax.ShapeDtypeStruct((t, nw), F32), jax.ShapeDtypeStruct((t, LANES), F32)],
        scratch_shapes=[pltpu.VMEM((tm, d), BF16)],
        compiler_params=_cparams(("parallel", "arbitrary")),
        name="in_proj",
    )(x, g, w_main, w_ba)


def _delta_prep_kernel(cur_ref, prev_ref, cbuf_ref, cw_ref, ba_ref, alog_ref, dtb_ref,
                       q_ref, k_ref, v_ref, beta_ref, gc_ref, *, tiles_per_seq, chunk):
    tm = cur_ref.shape[0]
    qkv_w = cur_ref.shape[1]
    d = qkv_w // 3
    first = (pl.program_id(0) % tiles_per_seq) == 0
    col_w = 4 * HEAD_W
    for c0 in range(0, qkv_w, col_w):
        cur = cur_ref[:, c0:c0 + col_w]
        prev = jnp.where(first, cbuf_ref[:, c0:c0 + col_w], prev_ref[:, c0:c0 + col_w])
        ext = jnp.concatenate([prev, cur], axis=0)
        conv = cw_ref[0:1, c0:c0 + col_w] * ext[SUBLANES - 3:SUBLANES - 3 + tm]
        for j in range(1, CONV_W):
            conv = conv + cw_ref[j:j + 1, c0:c0 + col_w] * ext[SUBLANES - 3 + j:SUBLANES - 3 + j + tm]
        act = _silu(conv)
        for hh in range(col_w // HEAD_W):
            col = c0 + hh * HEAD_W
            a = act[:, hh * HEAD_W:(hh + 1) * HEAD_W]
            if col < 2 * d:
                a = a * lax.rsqrt(jnp.sum(a * a, axis=-1, keepdims=True) + RMS_EPS)
            if col < d:
                q_ref[:, col:col + HEAD_W] = (a * (HEAD_W ** -0.5)).astype(BF16)
            elif col < 2 * d:
                k_ref[:, col - d:col - d + HEAD_W] = a.astype(BF16)
            else:
                v_ref[:, col - 2 * d:col - 2 * d + HEAD_W] = a.astype(BF16)
    ba = ba_ref[...]
    beta_ref[...] = _sigmoid(ba)
    xs = ba + dtb_ref[...]
    softplus = jnp.maximum(xs, 0.0) + jnp.log(1.0 + jnp.exp(-jnp.abs(xs)))
    g = -jnp.exp(alog_ref[...]) * softplus
    row = lax.broadcasted_iota(jnp.int32, (tm, tm), 0)
    colm = lax.broadcasted_iota(jnp.int32, (tm, tm), 1)
    tri = jnp.where((row // chunk == colm // chunk) & (colm <= row), 1.0, 0.0).astype(F32)
    gc_ref[...] = jnp.dot(tri, g, preferred_element_type=F32, precision=lax.Precision.HIGHEST)


def _delta_prep(proj, proj_ba, cbuf8, conv_w, alog_pad, dtb_pad, seq_len, tm, chunk):
    t = proj.shape[0]
    qkv_w = conv_w.shape[1]
    d = qkv_w // 3
    tiles_per_seq = seq_len // tm
    rows8 = tm // SUBLANES
    kern = functools.partial(_delta_prep_kernel, tiles_per_seq=tiles_per_seq, chunk=chunk)
    return pl.pallas_call(
        kern,
        grid=(t // tm,),
        in_specs=[
            pl.BlockSpec((tm, qkv_w), lambda i: (i, 0)),
            pl.BlockSpec((SUBLANES, qkv_w), lambda i: (jnp.maximum(i * rows8 - 1, 0), 0)),
            pl.BlockSpec((None, SUBLANES, qkv_w), lambda i: (i // tiles_per_seq, 0, 0)),
            pl.BlockSpec((CONV_W, qkv_w), lambda i: (0, 0)),
            pl.BlockSpec((tm, LANES), lambda i: (i, 0)),
            pl.BlockSpec((1, LANES), lambda i: (0, 0)),
            pl.BlockSpec((1, LANES), lambda i: (0, 0)),
        ],
        out_specs=[
            pl.BlockSpec((tm, d), lambda i: (i, 0)),
            pl.BlockSpec((tm, d), lambda i: (i, 0)),
            pl.BlockSpec((tm, d), lambda i: (i, 0)),
            pl.BlockSpec((tm, LANES), lambda i: (i, 0)),
            pl.BlockSpec((tm, LANES), lambda i: (i, 0)),
        ],
        out_shape=[jax.ShapeDtypeStruct((t, d), BF16)] * 3 + [jax.ShapeDtypeStruct((t, LANES), F32)] * 2,
        compiler_params=_cparams(("parallel",)),
        name="delta_prep",
    )(proj, proj, cbuf8, conv_w, proj_ba, alog_pad, dtb_pad)


V_AUG = HEAD_W + 16


def _attn_prep_kernel(fq_ref, fk_ref, fv_ref, gq_ref, gk_ref, q_ref, kf_ref, ka_ref, v_ref, vf_ref, *, transposed):
    tm, d = fq_ref.shape
    vf_ref[...] = fv_ref[...]
    lane = lax.broadcasted_iota(jnp.int32, (tm, HEAD_W), 1)
    low = lane < DIFF_QK

    def qk_norm(x, g):
        s = x * x
        tot = jnp.sum(s, axis=-1, keepdims=True)
        lo = jnp.sum(jnp.where(low, s, 0.0), axis=-1, keepdims=True)
        ms = jnp.where(low, lo, tot - lo) * (1.0 / DIFF_QK)
        return x * lax.rsqrt(ms + RMS_EPS) * g

    for h in range(d // HEAD_W):
        sl = slice(h * HEAD_W, (h + 1) * HEAD_W)
        qn = qk_norm(fq_ref[:, sl], gq_ref[...]) * (DIFF_QK ** -0.5)
        kn = qk_norm(fk_ref[:, sl], gk_ref[...])
        kf_ref[:, sl] = kn
        if transposed:
            ka_ref[h] = kn.astype(BF16)
            q_ref[sl, :] = (qn * LOG2_E).T.astype(BF16)
            v_ref[h, 0:HEAD_W, :] = fv_ref[:, sl].T.astype(BF16)
            v_ref[h, HEAD_W:V_AUG, :] = jnp.ones((V_AUG - HEAD_W, tm), BF16)
        else:
            ka_ref[:, sl] = kn.astype(BF16)
            q_ref[:, sl] = qn.astype(BF16)
            v_ref[:, sl] = fv_ref[:, sl].astype(BF16)


def _attn_prep(proj, gq2, gk2, d, tm, col_q, col_k, col_v, seq_len, transposed):
    t = proj.shape[0]
    n_heads = d // HEAD_W
    row = pl.BlockSpec((tm, d), lambda i: (i, 0))
    if transposed:
        per_seq = seq_len // tm
        q_spec = pl.BlockSpec((d, tm), lambda i: (0, i))
        q_shape = jax.ShapeDtypeStruct((d, t), BF16)
        v_spec = pl.BlockSpec((None, n_heads, None, V_AUG, tm), lambda i: (i // per_seq, 0, i % per_seq, 0, 0))
        v_shape = jax.ShapeDtypeStruct((t // seq_len, n_heads, per_seq, V_AUG, tm), BF16)
        k_spec = pl.BlockSpec((None, n_heads, tm, HEAD_W), lambda i: (i // per_seq, 0, i % per_seq, 0))
        k_shape = jax.ShapeDtypeStruct((t // seq_len, n_heads, seq_len, HEAD_W), BF16)
    else:
        q_spec, q_shape = row, jax.ShapeDtypeStruct((t, d), BF16)
        v_spec, v_shape = row, jax.ShapeDtypeStruct((t, d), BF16)
        k_spec, k_shape = row, jax.ShapeDtypeStruct((t, d), BF16)
    return pl.pallas_call(
        functools.partial(_attn_prep_kernel, transposed=transposed),
        grid=(t // tm,),
        in_specs=[
            pl.BlockSpec((tm, d), lambda i: (i, col_q)),
            pl.BlockSpec((tm, d), lambda i: (i, col_k)),
            pl.BlockSpec((tm, d), lambda i: (i, col_v)),
            pl.BlockSpec((1, HEAD_W), lambda i: (0, 0)),
            pl.BlockSpec((1, HEAD_W), lambda i: (0, 0)),
        ],
        out_specs=[q_spec, row, k_spec, v_spec, row],
        out_shape=[q_shape, jax.ShapeDtypeStruct((t, d), F32), k_shape, v_shape,
                   jax.ShapeDtypeStruct((t, d), F32)],
        compiler_params=_cparams(("parallel",)),
        name="attn_prep",
    )(proj, proj, proj, gq2, gk2)


def _delta_rule_kernel(q_ref, k_ref, v_ref, z_ref, beta_ref, gc_ref, s0_ref, ng_ref,
                       o_ref, sfin_ref, s_ref, *, chunk, n_chunks):
    n = pl.program_id(1)
    rows = q_ref.shape[0]
    n_heads = q_ref.shape[1] // HEAD_W

    @pl.when(n == 0)
    def _():
        s_ref[...] = s0_ref[...]

    lane = lax.broadcasted_iota(jnp.int32, (rows, LANES), 1)
    ri = lax.broadcasted_iota(jnp.int32, (chunk, chunk), 0)
    ci = lax.broadcasted_iota(jnp.int32, (chunk, chunk), 1)
    eye = ri == ci
    incl = ci <= ri
    strict = ci < ri
    eye_f = jnp.where(eye, 1.0, 0.0).astype(F32)
    beta_all = beta_ref[...]
    gc_all = gc_ref[...]
    n_stage = int(math.log2(chunk))
    heads = range(n_heads)
    chunks = range(rows // chunk)
    items = [(h, c) for c in chunks for h in heads]

    beta_h = [jnp.sum(jnp.where(lane == h, beta_all, 0.0), axis=-1, keepdims=True) for h in heads]
    g_h = [jnp.sum(jnp.where(lane == n_heads + h, gc_all, 0.0), axis=-1, keepdims=True) for h in heads]
    q16, k16, kq, decay, eg, glast, gcol, rhs = {}, {}, {}, {}, {}, {}, {}, {}
    for it in items:
        h, c = it
        sl = slice(h * HEAD_W, (h + 1) * HEAD_W)
        rs = slice(c * chunk, (c + 1) * chunk)
        q16[it] = q_ref[rs, sl]
        k16[it] = k_ref[rs, sl]
        k = k16[it].astype(F32)
        beta = beta_h[h][rs]
        gcol[it] = g_h[h][rs]
        grow = jnp.sum(jnp.where(eye, gcol[it], 0.0), axis=0, keepdims=True)
        decay[it] = jnp.where(incl, jnp.exp(jnp.where(incl, gcol[it] - grow, 0.0)), 0.0)
        eg[it] = jnp.exp(gcol[it])
        glast[it] = gcol[it][chunk - 1:chunk]
        kb = k * beta
        kq[it] = jnp.concatenate([kb.astype(BF16), q16[it]], axis=0)
        rhs[it] = jnp.concatenate([(kb * eg[it]).astype(BF16),
                                   (v_ref[rs, sl].astype(F32) * beta).astype(BF16)], axis=1)
    kk = {it: _dot_nt(kq[it], k16[it]) for it in items}
    mp = {it: -jnp.where(strict, kk[it][0:chunk] * decay[it], 0.0) for it in items}
    tmat = {it: eye_f + mp[it] for it in items}
    for stage in range(1, n_stage):
        m16 = {it: mp[it].astype(BF16) for it in items}
        mp = {it: _dot(m16[it], m16[it]) for it in items}
        m16 = {it: mp[it].astype(BF16) for it in items}
        tmat = {it: tmat[it] + _dot(m16[it], tmat[it].astype(BF16)) for it in items}
    wu = {it: _dot(tmat[it].astype(BF16), rhs[it]) for it in items}
    intra16 = {it: (kk[it][chunk:2 * chunk] * decay[it]).astype(BF16) for it in items}

    s = [s_ref[h] for h in heads]
    for c in chunks:
        s16 = [s[h].astype(BF16) for h in heads]
        lhs = [jnp.concatenate([wu[(h, c)][:, 0:HEAD_W].astype(BF16),
                                (q16[(h, c)].astype(F32) * eg[(h, c)]).astype(BF16)], axis=0) for h in heads]
        ws = [_dot(lhs[h], s16[h]) for h in heads]
        vn16 = [(wu[(h, c)][:, HEAD_W:2 * HEAD_W] - ws[h][0:chunk]).astype(BF16) for h in heads]
        o = [ws[h][chunk:2 * chunk] + _dot(intra16[(h, c)], vn16[h]) for h in heads]
        kd16 = [(k16[(h, c)].astype(F32) * jnp.exp(glast[(h, c)] - gcol[(h, c)])).astype(BF16) for h in heads]
        s = [s[h] * jnp.exp(glast[(h, c)]) + _dot_tn(kd16[h], vn16[h]) for h in heads]
        for h in heads:
            sl = slice(h * HEAD_W, (h + 1) * HEAD_W)
            rs = slice(c * chunk, (c + 1) * chunk)
            on = o[h] * lax.rsqrt(jnp.mean(o[h] * o[h], axis=-1, keepdims=True) + RMS_EPS) * ng_ref[...]
            o_ref[rs, sl] = (on * _silu(z_ref[rs, sl])).astype(BF16)
    for h in heads:
        s_ref[h] = s[h]

    @pl.when(n == n_chunks - 1)
    def _():
        sfin_ref[...] = s_ref[...]


def _delta_rule(dq, dk, dv, proj, beta, gc, s0, ng, seq_len, chunk, rows, col_z):
    t, d = dq.shape
    nb = t // seq_len
    n_steps = seq_len // rows
    n_heads = d // HEAD_W
    kern = functools.partial(_delta_rule_kernel, chunk=chunk, n_chunks=n_steps)
    tok = lambda b, n: (b * n_steps + n, 0)
    return pl.pallas_call(
        kern,
        grid=(nb, n_steps),
        in_specs=[
            pl.BlockSpec((rows, d), tok),
            pl.BlockSpec((rows, d), tok),
            pl.BlockSpec((rows, d), tok),
            pl.BlockSpec((rows, d), lambda b, n: (b * n_steps + n, col_z)),
            pl.BlockSpec((rows, LANES), tok),
            pl.BlockSpec((rows, LANES), tok),
            pl.BlockSpec((None, n_heads, HEAD_W, HEAD_W), lambda b, n: (b, 0, 0, 0)),
            pl.BlockSpec((1, HEAD_W), lambda b, n: (0, 0)),
        ],
        out_specs=[
            pl.BlockSpec((rows, d), tok),
            pl.BlockSpec((None, n_heads, HEAD_W, HEAD_W), lambda b, n: (b, 0, 0, 0)),
        ],
        out_shape=[jax.ShapeDtypeStruct((t, d), BF16),
                   jax.ShapeDtypeStruct((nb, n_heads, HEAD_W, HEAD_W), F32)],
        scratch_shapes=[pltpu.VMEM((n_heads, HEAD_W, HEAD_W), F32)],
        compiler_params=_cparams(("parallel", "arbitrary")),
        name="delta_rule",
    )(dq, dk, dv, proj, beta, gc, s0, ng)


def _lambda(lam_ref, lam_init):
    l = lam_ref[...]
    a = jnp.sum(l[0:1] * l[1:2], axis=-1, keepdims=True)
    b = jnp.sum(l[2:3] * l[3:4], axis=-1, keepdims=True)
    return jnp.exp(a) - jnp.exp(b) + lam_init


def _sub_norm(o, g, lam_init):
    return o * lax.rsqrt(jnp.mean(o * o, axis=-1, keepdims=True) + RMS_EPS) * g * (1.0 - lam_init)


def _attn_prompt_kernel(qt_ref, k_ref, vt_ref, lam_ref, ng_ref, o_ref, m_ref, acc_ref, sa_ref, sb_ref, *, lam_init):
    i = pl.program_id(2)
    bq = qt_ref.shape[1]
    bk = vt_ref.shape[-1]
    assert bq == 2 * bk
    qt = qt_ref[...]
    row = lax.broadcasted_iota(jnp.int32, qt.shape, 0)
    zero = jnp.zeros_like(qt)
    qm = (jnp.where(row < DIFF_QK, qt, zero), jnp.where(row >= DIFF_QK, qt, zero))
    m_ref[...] = jnp.full(m_ref.shape, NEG_INF, F32)
    acc_ref[...] = jnp.zeros(acc_ref.shape, F32)

    full = slice(0, bq)
    half = slice(bk, bq)
    qm_half = tuple(x[:, half] for x in qm)

    def scores(j, s_ref, qs=full):
        kb = k_ref[pl.ds(pl.multiple_of(j * bk, bk), bk), :]
        for m in range(2):
            s_ref[m, :, qs] = _dot(kb, qm[m] if qs is full else qm_half[m])

    def absorb(j, s_ref, mask=None, qs=full):
        vb = vt_ref[j]
        for m in range(2):
            s = s_ref[m, :, qs]
            if mask is not None:
                s = jnp.where(mask, s, NEG_INF)
            m_prev = m_ref[m, :, qs]
            m_new = jnp.maximum(m_prev, jnp.max(s, axis=0, keepdims=True))
            alpha = jnp.exp2(m_prev - m_new)
            p = jnp.exp2(s - m_new[0:1]).astype(BF16)
            acc_ref[m, :, qs] = alpha[0:1] * acc_ref[m, :, qs] + _dot(vb, p)
            m_ref[m, :, qs] = m_new

    scores(0, sa_ref)

    def pair(j0):
        scores(j0 + 1, sb_ref)
        absorb(j0, sa_ref)
        scores(j0 + 2, sa_ref)
        absorb(j0 + 1, sb_ref)

    def body(g, carry):
        pair(4 * g)
        pair(4 * g + 2)
        return carry

    lax.fori_loop(0, i // 2, body, 0)

    @pl.when(i % 2 == 1)
    def _():
        pair(2 * i - 2)

    scores(2 * i + 1, sb_ref, half)
    ri = lax.broadcasted_iota(jnp.int32, (bk, bq), 0)
    ci = lax.broadcasted_iota(jnp.int32, (bk, bq), 1)
    absorb(2 * i, sa_ref, (ri // CHUNK) <= (ci // CHUNK))
    rh = lax.broadcasted_iota(jnp.int32, (bk, bk), 0)
    ch = lax.broadcasted_iota(jnp.int32, (bk, bk), 1)
    absorb(2 * i + 1, sb_ref, (rh // CHUNK) <= (ch // CHUNK), half)

    lam = _lambda(lam_ref, lam_init)
    a0 = acc_ref[0]
    a1 = acc_ref[1]
    ot = a0[0:HEAD_W] / a0[HEAD_W:HEAD_W + 1] - lam * (a1[0:HEAD_W] / a1[HEAD_W:HEAD_W + 1])
    o_ref[...] = _sub_norm(ot.T, ng_ref[...], lam_init).astype(BF16)


def _attn_prompt(qt, ka, vt, lam4, ng, seq_len, bk, lam_init):
    d, t = qt.shape
    nb = t // seq_len
    bq = 2 * bk
    nq = seq_len // bq
    nk = seq_len // bk
    n_heads = d // HEAD_W
    kern = functools.partial(_attn_prompt_kernel, lam_init=lam_init)
    return pl.pallas_call(
        kern,
        grid=(nb, n_heads, nq),
        in_specs=[
            pl.BlockSpec((HEAD_W, bq), lambda b, h, i: (h, b * nq + i)),
            pl.BlockSpec((None, None, seq_len, HEAD_W), lambda b, h, i: (b, h, 0, 0)),
            pl.BlockSpec((None, None, nk, V_AUG, bk), lambda b, h, i: (b, h, 0, 0, 0)),
            pl.BlockSpec((4, DIFF_QK), lambda b, h, i: (0, 0)),
            pl.BlockSpec((1, HEAD_W), lambda b, h, i: (0, 0)),
        ],
        out_specs=pl.BlockSpec((bq, HEAD_W), lambda b, h, i: (b * nq + i, h)),
        out_shape=jax.ShapeDtypeStruct((t, d), BF16),
        scratch_shapes=[pltpu.VMEM((2, SUBLANES, bq), F32), pltpu.VMEM((2, V_AUG, bq), F32),
                        pltpu.VMEM((2, bk, bq), F32), pltpu.VMEM((2, bk, bq), F32)],
        compiler_params=_cparams(("parallel", "parallel", "arbitrary")),
        name="attn_prompt",
    )(qt, ka, vt, lam4, ng)


def _attn_cached_kernel(q_ref, kn_ref, vn_ref, ck_ref, cv_ref, lam_ref, ng_ref, o_ref, *, lam_init):
    rows = q_ref.shape[0]
    lane = lax.broadcasted_iota(jnp.int32, (rows, HEAD_W), 1)
    lam = _lambda(lam_ref, lam_init)
    for h in range(q_ref.shape[1] // HEAD_W):
        sl = slice(h * HEAD_W, (h + 1) * HEAD_W)
        q = q_ref[:, sl]
        zero = jnp.zeros_like(q)
        kp = ck_ref[:, sl].astype(BF16)
        vp = cv_ref[:, sl].astype(BF16)
        kn = kn_ref[:, sl]
        vn = vn_ref[:, sl]
        outs = []
        for m in range(2):
            qm = jnp.where((lane < DIFF_QK) if m == 0 else (lane >= DIFF_QK), q, zero)
            sp = _dot_nt(qm, kp)
            sn = _dot_nt(qm, kn)
            mx = jnp.maximum(jnp.max(sp, axis=-1, keepdims=True), jnp.max(sn, axis=-1, keepdims=True))
            pp = jnp.exp(sp - mx)
            pn = jnp.exp(sn - mx)
            den = jnp.sum(pp, axis=-1, keepdims=True) + jnp.sum(pn, axis=-1, keepdims=True)
            outs.append((_dot(pp.astype(BF16), vp) + _dot(pn.astype(BF16), vn)) / den)
        o = outs[0] - lam * outs[1]
        o_ref[:, sl] = _sub_norm(o, ng_ref[...], lam_init).astype(BF16)


def _attn_cached(qa, ka, va, cache_k, cache_v, lam4, ng, seq_len, lam_init):
    t, d = qa.shape
    nb = t // seq_len
    past = cache_k.shape[1]
    kern = functools.partial(_attn_cached_kernel, lam_init=lam_init)
    tok = pl.BlockSpec((seq_len, d), lambda b: (b, 0))
    return pl.pallas_call(
        kern,
        grid=(nb,),
        in_specs=[
            tok, tok, tok,
            pl.BlockSpec((None, past, d), lambda b: (b, 0, 0)),
            pl.BlockSpec((None, past, d), lambda b: (b, 0, 0)),
            pl.BlockSpec((4, DIFF_QK), lambda b: (0, 0)),
            pl.BlockSpec((1, HEAD_W), lambda b: (0, 0)),
        ],
        out_specs=tok,
        out_shape=jax.ShapeDtypeStruct((t, d), BF16),
        compiler_params=_cparams(("parallel",)),
        name="attn_cached",
    )(qa, ka, va, cache_k, cache_v, lam4, ng)


def _merge_out_kernel(x_ref, oa_ref, ob_ref, ga_ref, gb_ref, wo_ref, g2_ref, h_ref, xnt_ref):
    merged = (_sigmoid(ga_ref[...]) * oa_ref[...].astype(F32)
              + _sigmoid(gb_ref[...]) * ob_ref[...].astype(F32))
    h = x_ref[...] + _dot(merged.astype(BF16), wo_ref[...])
    h_ref[...] = h
    hn = h * lax.rsqrt(jnp.mean(h * h, axis=-1, keepdims=True) + RMS_EPS) * g2_ref[...]
    xnt_ref[...] = hn.T.astype(BF16)


def _merge_out(x, oa, ob, proj, w_out, g2, tm, col_ga, col_gb):
    t, d = x.shape
    row = lambda i: (i, 0)
    return pl.pallas_call(
        _merge_out_kernel,
        grid=(t // tm,),
        in_specs=[
            pl.BlockSpec((tm, d), row),
            pl.BlockSpec((tm, d), row),
            pl.BlockSpec((tm, d), row),
            pl.BlockSpec((tm, d), lambda i: (i, col_ga)),
            pl.BlockSpec((tm, d), lambda i: (i, col_gb)),
            pl.BlockSpec((d, d), lambda i: (0, 0)),
            pl.BlockSpec((1, d), lambda i: (0, 0)),
        ],
        out_specs=[pl.BlockSpec((tm, d), row), pl.BlockSpec((d, tm), lambda i: (0, i))],
        out_shape=[jax.ShapeDtypeStruct((t, d), F32), jax.ShapeDtypeStruct((d, t), BF16)],
        compiler_params=_cparams(("parallel",)),
        name="merge_out",
    )(x, oa, ob, proj, proj, w_out, g2)


def _top_values(parts, count):
    vals = []
    for r in range(count):
        m = parts[0]
        for p in parts[1:]:
            m = jnp.maximum(m, p)
        m = jnp.max(m, axis=0, keepdims=True)
        vals.append(m)
        if r + 1 < count:
            parts = [jnp.where(p == m, NEG_INF, p) for p in parts]
    return vals


def _sorting_network(n):
    pairs = []
    p = 1
    while p < n:
        k = p
        while k >= 1:
            for j in range(k % p, n - k, 2 * k):
                for i in range(min(k, n - j - k)):
                    if (i + j) // (2 * p) == (i + j + k) // (2 * p):
                        pairs.append((i + j, i + j + k))
            k //= 2
        p *= 2
    return pairs


def _top_values_sorted(x, count):
    m = x.shape[0] // SUBLANES
    assert count <= m
    tiles = [x[SUBLANES * v:SUBLANES * (v + 1)] for v in range(m)]
    for i, j in _sorting_network(m):
        tiles[i], tiles[j] = jnp.maximum(tiles[i], tiles[j]), jnp.minimum(tiles[i], tiles[j])
    sub = lax.broadcasted_iota(jnp.int32, tiles[0].shape, 0).astype(F32)
    vals = []
    for r in range(count):
        top = jnp.max(tiles[0], axis=0, keepdims=True)
        vals.append(top)
        live = count - r - 1
        if live > 0:
            first = jnp.min(jnp.where(tiles[0] == top, sub, float(SUBLANES)), axis=0, keepdims=True)
            pop = sub == first
            for v in range(live):
                tiles[v] = jnp.where(pop, tiles[v + 1], tiles[v])
    return vals


def _pair_candidates(sv0, sv1, count):
    k = sv0.shape[0]
    parts = []
    first_single = k
    for i in range(k):
        nj = min(k, count // (i + 1))
        if nj > 1:
            for j0 in range(0, nj, SUBLANES):
                parts.append(sv0[i:i + 1] + sv1[j0:j0 + SUBLANES])
        else:
            first_single = min(first_single, i)
    for i0 in range(first_single, k, SUBLANES):
        parts.append(sv0[i0:i0 + SUBLANES] + sv1[0:1])
    return parts


def _peer_route_kernel(xnt_ref, wqt_ref, sk_ref, s0_ref, s1_ref, tau_ref):
    pq = _dot(wqt_ref[...], xnt_ref[...]).astype(BF16)
    dh = sk_ref.shape[-1]
    taus = []
    for h in range(PEER_HEADS):
        sc = []
        for p in range(2):
            r0 = (h * 2 + p) * dh
            sc.append(_dot(sk_ref[h, p], pq[r0:r0 + dh]))
        sv0 = jnp.concatenate(_top_values_sorted(sc[0], PEER_TOPK), axis=0)
        sv1 = jnp.concatenate(_top_values_sorted(sc[1], PEER_TOPK), axis=0)
        tv = _top_values(_pair_candidates(sv0, sv1, PEER_TOPK + 1), PEER_TOPK + 1)
        mx = tv[0]
        z = jnp.exp(tv[0] - mx)
        for r in range(1, PEER_TOPK):
            z = z + jnp.exp(tv[r] - mx)
        logz = jnp.log(z)
        taus.append((0.5 * (tv[PEER_TOPK - 1] + tv[PEER_TOPK]) - mx - logz) * LOG2_E)
        s0_ref[h] = (sc[0] - sv0[0:1] - logz) * LOG2_E
        s1_ref[h] = (sc[1] - sv1[0:1]) * LOG2_E
    tau_ref[...] = jnp.concatenate(taus, axis=0)


def _peer_route(xnt, wqt, sk16, tm):
    d, t = xnt.shape
    qw = wqt.shape[0]
    tok = lambda i: (0, 0, i)
    return pl.pallas_call(
        _peer_route_kernel,
        grid=(t // tm,),
        in_specs=[
            pl.BlockSpec((d, tm), lambda i: (0, i)),
            pl.BlockSpec((qw, d), lambda i: (0, 0)),
            pl.BlockSpec(sk16.shape, lambda i: (0, 0, 0, 0)),
        ],
        out_specs=[
            pl.BlockSpec((PEER_HEADS, N_KEYS, tm), tok),
            pl.BlockSpec((PEER_HEADS, N_KEYS, tm), tok),
            pl.BlockSpec((PEER_HEADS, tm), lambda i: (0, i)),
        ],
        out_shape=[jax.ShapeDtypeStruct((PEER_HEADS, N_KEYS, t), F32),
                   jax.ShapeDtypeStruct((PEER_HEADS, N_KEYS, t), F32),
                   jax.ShapeDtypeStruct((PEER_HEADS, t), F32)],
        compiler_params=_cparams(("parallel",)),
        name="peer_route",
    )(xnt, wqt, sk16)


def _gelu(x):
    return 0.5 * x * (1.0 + lax.erf(x * (2.0 ** -0.5)))


PEER_EXPERT_BLOCK = 1024
GROUP_ROWS = 64


BF16_ROWS = 16


def _peer_mix_kernel(xnt_ref, u_ref, vt_ref, s0_ref, s1_ref, tau_ref, h_ref, y_ref, acc_ref, e1_ref, *, n_eblk):
    j = pl.program_id(1)
    na = s0_ref.shape[1]
    tm = xnt_ref.shape[1]

    @pl.when(j == 0)
    def _():
        acc_ref[...] = jnp.zeros(acc_ref.shape, F32)
        e1_ref[...] = jnp.exp2(s1_ref[...])

    xnt = xnt_ref[...]
    sub = 2 * N_KEYS
    n_sub = na * N_KEYS // sub
    lw = min(tm, LANES)
    grp = GROUP_ROWS // BF16_ROWS

    def activations(k):
        return _dot(u_ref[k * sub:(k + 1) * sub, :], xnt)

    def row16(x):
        return x[None]

    act = activations(0)
    out = None
    for k in range(n_sub):
        act_next = activations(k + 1) if k + 1 < n_sub else None
        a_lo, a_hi = k * sub // N_KEYS, (k + 1) * sub // N_KEYS
        piece = {}
        for t0 in range(0, tm, lw):
            ts = slice(t0, t0 + lw)
            s0 = [s0_ref[h, a_lo:a_hi, ts] for h in range(PEER_HEADS)]
            e0 = [jnp.exp2(s0[h]) for h in range(PEER_HEADS)]
            e1_min = [jnp.exp2(tau_ref[h:h + 1, ts] - s0[h]) for h in range(PEER_HEADS)]
            for a in range(a_lo, a_hi):
                ar = slice(a - a_lo, a - a_lo + 1)
                rows = [(row16(e1_min[h][ar]), row16(e0[h][ar])) for h in range(PEER_HEADS)]
                for b0 in range(0, N_KEYS, GROUP_ROWS):
                    r0 = a * N_KEYS + b0 - k * sub
                    wgt = jnp.zeros((grp, BF16_ROWS, lw), F32)
                    for h in range(PEER_HEADS):
                        e1 = e1_ref[h, b0:b0 + GROUP_ROWS, ts].reshape(grp, BF16_ROWS, lw)
                        wgt = wgt + jnp.where(e1 >= rows[h][0], e1, jnp.zeros_like(e1)) * rows[h][1]
                    gate = _gelu(act[r0:r0 + GROUP_ROWS, ts])
                    piece[(r0, t0)] = (wgt.reshape(GROUP_ROWS, lw) * gate).astype(BF16)
        gw = jnp.concatenate([jnp.concatenate([piece[(r0, t0)] for t0 in range(0, tm, lw)], axis=1)
                              for r0 in range(0, sub, GROUP_ROWS)], axis=0)
        part = _dot(vt_ref[:, k * sub:(k + 1) * sub], gw)
        out = part if out is None else out + part
        act = act_next
    acc_ref[...] += out

    @pl.when(j == n_eblk - 1)
    def _():
        y_ref[...] = h_ref[...] + acc_ref[...].T


def _peer_mix(xnt, u16, vt16, s0z, s1, tau, h, tm):
    d, t = xnt.shape
    n_exp = u16.shape[0]
    te = vt16.shape[2]
    na = te // N_KEYS
    n_eblk = n_exp // te
    kern = functools.partial(_peer_mix_kernel, n_eblk=n_eblk)
    return pl.pallas_call(
        kern,
        grid=(t // tm, n_eblk),
        in_specs=[
            pl.BlockSpec((d, tm), lambda i, j: (0, i)),
            pl.BlockSpec((te, d), lambda i, j: (j, 0)),
            pl.BlockSpec((None, d, te), lambda i, j: (j, 0, 0)),
            pl.BlockSpec((PEER_HEADS, na, tm), lambda i, j: (0, j, i)),
            pl.BlockSpec((PEER_HEADS, N_KEYS, tm), lambda i, j: (0, 0, i)),
            pl.BlockSpec((PEER_HEADS, tm), lambda i, j: (0, i)),
            pl.BlockSpec((tm, d), lambda i, j: (i, 0)),
        ],
        out_specs=pl.BlockSpec((tm, d), lambda i, j: (i, 0)),
        out_shape=jax.ShapeDtypeStruct((t, d), F32),
        scratch_shapes=[pltpu.VMEM((d, tm), F32), pltpu.VMEM((PEER_HEADS, N_KEYS, tm), F32)],
        compiler_params=_cparams(("parallel", "arbitrary")),
        name="peer_mix",
    )(xnt, u16, vt16, s0z, s1, tau, h)


def _pick(n, pref):
    t = min(n, pref)
    while n % t:
        t //= 2
    return t


def _layer(x, conv_buf, s0, past_k, past_v, lam_init, wts):
    nb, seq_len, d = x.shape
    t = nb * seq_len
    n_heads = d // HEAD_W
    qkv_w = 3 * d
    x2 = x.reshape(t, d)

    tm = _pick(t, 2048)
    proj, proj_ba = _in_proj(x2, wts["norm_mix_g"], wts["w_main"], wts["w_ba"], tm, 1024)
    col ={name: idx for idx, name in enumerate(("q", "k", "v", "z", "fq", "fk", "fv", "ga", "gb"))}

    chunk = min(CHUNK, seq_len)
    tp = _pick(seq_len, 256)
    cbuf8 = jnp.concatenate(
        [jnp.zeros((nb, SUBLANES - (CONV_W - 1), qkv_w), F32), conv_buf.astype(F32)], axis=1)
    dq, dk, dv, beta, gc = _delta_prep(proj, proj_ba, cbuf8, wts["conv_w"], wts["alog_pad"], wts["dtb_pad"],
                                       seq_len, tp, chunk)
    prompt = past_k is None
    bk = _pick(seq_len // 2, 512)
    qa, kf, ka, va, vf = _attn_prep(proj, wts["gq2"], wts["gk2"], d, bk if prompt else _pick(t, 256),
                                col["fq"], col["fk"], col["fv"], seq_len, prompt)

    rows = min(seq_len, 4 * chunk)
    oa, s_new = _delta_rule(dq, dk, dv, proj, beta, gc, s0, wts["delta_norm_g"], seq_len, chunk, rows, col["z"])

    if prompt:
        ob = _attn_prompt(qa, ka, va, wts["lam4"], wts["diff_norm_g"], seq_len, bk, lam_init)
    else:
        ob = _attn_cached(qa, ka, va, past_k.reshape(nb, past_k.shape[1], d),
                          past_v.reshape(nb, past_v.shape[1], d), wts["lam4"], wts["diff_norm_g"],
                          seq_len, lam_init)

    tmo = _pick(t, 256)
    h, xnt = _merge_out(x2, oa, ob, proj, wts["w_out"], wts["norm_ffn_g"], tmo, col["ga"], col["gb"])
    s0z, s1, tau = _peer_route(xnt, wts["wqt"], wts["sk"], _pick(t, 256))
    y = _peer_mix(xnt, wts["u"], wts["vt"], s0z, s1, tau, h, _pick(t, 512))

    new_k = kf.reshape(nb, seq_len, n_heads, 2, DIFF_QK)
    new_v = vf.reshape(nb, seq_len, n_heads, HEAD_W)
    assert seq_len >= CONV_W - 1
    new_conv = proj.reshape(nb, seq_len, proj.shape[1])[:, seq_len - (CONV_W - 1):, :qkv_w]
    return y.reshape(nb, seq_len, d), new_k, new_v, s_new, new_conv


def _prep_weights(l, d, norm_mix_g, w_in, conv_w, a_log, dt_bias, delta_norm_g, q_norm_g, k_norm_g,
                  lq1, lk1, lq2, lk2, diff_norm_g, w_out, norm_ffn_g, peer_w_q, peer_sub_keys, peer_u, peer_v):
    n_heads = d // HEAD_W
    qkv_w = 3 * d
    w = w_in[l]
    o_z = qkv_w
    o_beta = o_z + d
    o_a = o_beta + n_heads
    o_fq = o_a + n_heads
    w_main = jnp.concatenate([w[:, :o_beta], w[:, o_fq:]], axis=1).astype(BF16)
    w_ba = jnp.pad(w[:, o_beta:o_fq], ((0, 0), (0, LANES - 2 * n_heads))).astype(BF16)

    def lane_pad(v):
        return jnp.pad(v[l].astype(F32), (n_heads, LANES - 2 * n_heads)).reshape(1, LANES)

    return {
        "norm_mix_g": norm_mix_g[l].reshape(1, d),
        "w_main": w_main,
        "w_ba": w_ba,
        "conv_w": conv_w[l],
        "alog_pad": lane_pad(a_log),
        "dtb_pad": lane_pad(dt_bias),
        "delta_norm_g": delta_norm_g[l].reshape(1, HEAD_W),
        "gq2": jnp.tile(q_norm_g[l], 2).reshape(1, HEAD_W),
        "gk2": jnp.tile(k_norm_g[l], 2).reshape(1, HEAD_W),
        "lam4": jnp.stack([lq1[l], lk1[l], lq2[l], lk2[l]]).astype(F32),
        "diff_norm_g": diff_norm_g[l].reshape(1, HEAD_W),
        "w_out": w_out[l].astype(BF16),
        "norm_ffn_g": norm_ffn_g[l].reshape(1, d),
        "wqt": peer_w_q[l].T.astype(BF16),
        "sk": peer_sub_keys[l].astype(BF16),
        "u": peer_u[l].astype(BF16),
        "vt": peer_v[l].reshape(-1, PEER_EXPERT_BLOCK, d).transpose(0, 2, 1).astype(BF16),
    }


def kernel(x_prompt, x_sample, cache_diff_k, cache_diff_v, state_delta_s, state_delta_conv, norm_mix_g, w_in, conv_w, delta_a_log, delta_dt_bias, delta_norm_g, diff_q_norm_g, diff_k_norm_g, diff_lambda_q1, diff_lambda_k1, diff_lambda_q2, diff_lambda_k2, diff_norm_g, w_out, norm_ffn_g, peer_w_q, peer_sub_keys, peer_u, peer_v):
    depth = w_in.shape[0]
    d = x_prompt.shape[-1]
    n_heads = d // HEAD_W
    yp, ys = x_prompt, x_sample
    outs_p, outs_s = [], []
    for l in range(depth):
        wts = _prep_weights(l, d, norm_mix_g, w_in, conv_w, delta_a_log, delta_dt_bias, delta_norm_g,
                            diff_q_norm_g, diff_k_norm_g, diff_lambda_q1, diff_lambda_k1, diff_lambda_q2,
                            diff_lambda_k2, diff_norm_g, w_out, norm_ffn_g, peer_w_q, peer_sub_keys,
                            peer_u, peer_v)
        lam_init = 0.8 - 0.6 * math.exp(-0.3 * l)
        conv0 = jnp.zeros((yp.shape[0], CONV_W - 1, 3 * d), F32)
        s0 = jnp.zeros((yp.shape[0], n_heads, HEAD_W, HEAD_W), F32)
        yp, *rest = _layer(yp, conv0, s0, None, None, lam_init, wts)
        outs_p.append(rest)
        ys, *rest = _layer(ys, state_delta_conv[l], state_delta_s[l], cache_diff_k[l], cache_diff_v[l],
                           lam_init, wts)
        outs_s.append(rest)
    stack = lambda outs, i: jnp.stack([o[i] for o in outs])
    return (yp, ys,
            stack(outs_p, 0), stack(outs_p, 1), stack(outs_p, 2), stack(outs_p, 3),
            stack(outs_s, 0), stack(outs_s, 1), stack(outs_s, 2), stack(outs_s, 3))
```
